```python
import jax, jax.numpy as jnp
from jax import lax
import numpy as np

D_MODEL = 2048
BATCH = 8
SEQ = 2048
DEPTH = 2
DEC_BATCH = 32
DEC_SEQ = 8
PAST_LEN = 8192
PAGE_SIZE = 128

HEAD_DIM = 128
N_HEADS = D_MODEL // HEAD_DIM
H_SB = N_HEADS // 4
H_FOX = (N_HEADS - H_SB) // 2
H_DN = N_HEADS - H_SB - H_FOX
DN_W = H_DN * HEAD_DIM
FOX_W = H_FOX * HEAD_DIM
SB_W = H_SB * HEAD_DIM
MIX_W = DN_W + FOX_W + SB_W
DN_CONV = 4
DN_CHUNK = 64
Q_BLOCK = 128
D_FF = 256 * ((8 * D_MODEL // 3 + 255) // 256)
FFN_CONV = 3
EPS = 1e-6
IN_SIZES = (3 * DN_W, H_DN, H_DN, DN_W, FOX_W, FOX_W, FOX_W, H_FOX, SB_W, SB_W, SB_W)
IN_COLS = sum(IN_SIZES)

kernel_name = 'hybrid_deltanet_fox_stickbreak_step'

F32 = jnp.float32


def rmsnorm(x, w):
    xf = x.astype(F32)
    y = xf * lax.rsqrt(jnp.mean(xf * xf, axis=-1, keepdims=True) + EPS)
    return (y * w.astype(F32)).astype(x.dtype)


def l2norm(x):
    xf = x.astype(F32)
    return xf * lax.rsqrt(jnp.sum(xf * xf, axis=-1, keepdims=True) + EPS)


def split_cols(a, sizes):
    out, start = [], 0
    for s in sizes:
        out.append(a[..., start:start + s])
        start += s
    return out


def causal_dwconv(x, buf, w):
    k_w = w.shape[0]
    t = x.shape[1]
    xc = jnp.concatenate([buf.astype(x.dtype), x], axis=1)
    w = w.astype(x.dtype)
    y = xc[:, 0:t] * w[0]
    for i in range(1, k_w):
        y = y + xc[:, i:i + t] * w[i]
    return y, xc[:, xc.shape[1] - (k_w - 1):]


def gated_delta_rule(q, k, v, beta, g, s0):
    b, t, h, dk = q.shape
    dv = v.shape[-1]
    c = min(DN_CHUNK, t)
    n = -(-t // c)
    pad = n * c - t

    def blk(a):
        a = jnp.pad(a.astype(F32), [(0, 0), (0, pad)] + [(0, 0)] * (a.ndim - 2))
        a = a.reshape(b, n, c, *a.shape[2:])
        return a.transpose((1, 0, 3, 2) + tuple(range(4, a.ndim)))

    qc, kc, vc, bc, gc = blk(q), blk(k), blk(v), blk(beta), blk(g)
    gcum = jnp.cumsum(gc, axis=-1)
    idx = jnp.arange(c)
    lower = idx[:, None] >= idx[None, :]
    strict = idx[:, None] > idx[None, :]
    gamma = jnp.exp(jnp.where(lower, gcum[..., :, None] - gcum[..., None, :], -jnp.inf))
    kb = kc * bc[..., None]
    a_mat = jnp.where(strict, jnp.einsum('nbhid,nbhjd->nbhij', kb, kc) * gamma, 0.0) + jnp.eye(c, dtype=F32)
    rhs = jnp.concatenate([kb * jnp.exp(gcum)[..., None], vc * bc[..., None]], axis=-1)
    sol = lax.linalg.triangular_solve(a_mat, rhs, left_side=True, lower=True, unit_diagonal=True)
    w_c, u_c = sol[..., :dk], sol[..., dk:]
    qk = jnp.einsum('nbhid,nbhjd->nbhij', qc, kc) * gamma

    def step(s, xs):
        q_i, k_i, u_i, w_i, g_i, qk_i = xs
        v_new = u_i - jnp.einsum('bhck,bhkv->bhcv', w_i, s)
        o_i = (jnp.einsum('bhck,bhkv->bhcv', q_i * jnp.exp(g_i)[..., None], s)
               + jnp.einsum('bhij,bhjv->bhiv', qk_i, v_new))
        g_last = g_i[..., -1]
        k_dec = k_i * jnp.exp(g_last[..., None] - g_i)[..., None]
        s = s * jnp.exp(g_last)[..., None, None] + jnp.einsum('bhck,bhcv->bhkv', k_dec, v_new)
        return s, o_i

    s_fin, o = lax.scan(step, s0.astype(F32), (qc, kc, u_c, w_c, gcum, qk))
    o = o.transpose(1, 0, 3, 2, 4).reshape(b, n * c, h, dv)[:, :t]
    return o, s_fin


def query_sweep(block_fn, q_args, qpos):
    t = qpos.shape[0]
    if t <= Q_BLOCK:
        return block_fn(*q_args, qpos)
    nb = t // Q_BLOCK
    blocks = tuple(jnp.swapaxes(a.reshape(a.shape[0], nb, Q_BLOCK, *a.shape[2:]), 0, 1) for a in q_args)
    out = lax.map(lambda xs: block_fn(*xs), blocks + (qpos.reshape(nb, Q_BLOCK),))
    out = jnp.swapaxes(out, 0, 1)
    return out.reshape(out.shape[0], t, *out.shape[3:])


def fox_block(q, k, v, cq, ck, qpos, kpos):
    s = jnp.einsum('bqhd,bkhd->bhqk', q, k).astype(F32) * (HEAD_DIM ** -0.5)
    s = s + jnp.swapaxes(cq, 1, 2)[..., :, None] - jnp.swapaxes(ck, 1, 2)[..., None, :]
    s = jnp.where(kpos[None, :] <= qpos[:, None], s, -jnp.inf)
    p = jax.nn.softmax(s, axis=-1)
    return jnp.einsum('bhqk,bkhd->bqhd', p.astype(v.dtype), v)


def sb_block(q, k, v, qpos, kpos):
    z = jnp.einsum('bqhd,bkhd->bhqk', q, k).astype(F32) * (HEAD_DIM ** -0.5)
    mask = kpos[None, :] < qpos[:, None]
    log_rest = jnp.where(mask, jax.nn.log_sigmoid(-z), 0.0)
    suffix = lax.cumsum(log_rest, axis=log_rest.ndim - 1, reverse=True) - log_rest
    a = jnp.where(mask, jnp.exp(jax.nn.log_sigmoid(z) + suffix), 0.0)
    return jnp.einsum('bhqk,bkhd->bqhd', a.astype(v.dtype), v)


def fox_mixer(q, k, v, logf, past):
    t = q.shape[1]
    if past is not None:
        k = jnp.concatenate([past[0].astype(k.dtype), k], axis=1)
        v = jnp.concatenate([past[1].astype(v.dtype), v], axis=1)
        logf = jnp.concatenate([past[2].astype(F32), logf], axis=1)
    p = k.shape[1] - t
    c = jnp.cumsum(logf, axis=1)
    kpos = jnp.arange(p + t)
    qpos = p + jnp.arange(t)
    return query_sweep(lambda qb, cqb, qp: fox_block(qb, k, v, cqb, c, qp, kpos), (q, c[:, p:]), qpos)


def sb_mixer(q, k, v, past):
    t = q.shape[1]
    if past is not None:
        k = jnp.concatenate([past[0].astype(k.dtype), k], axis=1)
        v = jnp.concatenate([past[1].astype(v.dtype), v], axis=1)
    p = k.shape[1] - t
    kpos = jnp.arange(p + t)
    qpos = p + jnp.arange(t)
    return query_sweep(lambda qb, qp: sb_block(qb, k, v, qp, kpos), (q,), qpos)


def gather_pages(pool, page_table):
    g = pool[page_table]
    return g.reshape(g.shape[0], g.shape[1] * g.shape[2], *g.shape[3:])


def trunk_layer(x, lw, dn_s0, dn_buf, ffn_buf, fox_past, sb_past):
    b, t, _ = x.shape
    h = rmsnorm(x, lw['norm_pre_mix'])
    proj = h @ lw['w_in'].astype(h.dtype)
    (dn_qkv, dn_b, dn_a, dn_z, fq, fk, fv, ff, sq, sk, sv) = split_cols(proj, IN_SIZES)

    conv_out, dn_buf_new = causal_dwconv(dn_qkv, dn_buf, lw['dn_conv'])
    conv_out = jax.nn.silu(conv_out)
    qa, ka, va = split_cols(conv_out, (DN_W, DN_W, DN_W))
    qa = l2norm(qa.reshape(b, t, H_DN, HEAD_DIM)) * (HEAD_DIM ** -0.5)
    ka = l2norm(ka.reshape(b, t, H_DN, HEAD_DIM))
    va = va.reshape(b, t, H_DN, HEAD_DIM)
    beta = jax.nn.sigmoid(dn_b.astype(F32))
    g = -jnp.exp(lw['dn_A_log'].astype(F32)) * jax.nn.softplus(dn_a.astype(F32) + lw['dn_dt_bias'].astype(F32))
    o_a, s_new = gated_delta_rule(qa, ka, va, beta, g, dn_s0)
    o_a = (rmsnorm(o_a, lw['dn_norm']) * jax.nn.silu(dn_z.reshape(b, t, H_DN, HEAD_DIM).astype(F32))).astype(x.dtype)

    fq = fq.reshape(b, t, H_FOX, HEAD_DIM)
    fk = fk.reshape(b, t, H_FOX, HEAD_DIM)
    fv = fv.reshape(b, t, H_FOX, HEAD_DIM)
    logf = jax.nn.log_sigmoid(ff.astype(F32) + lw['fox_b_f'].astype(F32))
    o_b = rmsnorm(fox_mixer(fq, fk, fv, logf, fox_past), lw['fox_norm']).astype(x.dtype)

    sq = sq.reshape(b, t, H_SB, HEAD_DIM)
    sk = sk.reshape(b, t, H_SB, HEAD_DIM)
    sv = sv.reshape(b, t, H_SB, HEAD_DIM)
    o_c = rmsnorm(sb_mixer(sq, sk, sv, sb_past), lw['sb_norm']).astype(x.dtype)

    mix = jnp.concatenate([o_a.reshape(b, t, DN_W), o_b.reshape(b, t, FOX_W), o_c.reshape(b, t, SB_W)], axis=-1)
    x = x + rmsnorm(mix @ lw['w_out'].astype(mix.dtype), lw['norm_post_mix'])

    hf = rmsnorm(x, lw['norm_pre_ffn'])
    u = hf @ lw['ffn_w_up'].astype(hf.dtype)
    u, ffn_buf_new = causal_dwconv(u, ffn_buf, lw['ffn_conv'])
    gate, up = split_cols(u, (D_FF, D_FF))
    f = (jax.nn.silu(gate) * up) @ lw['ffn_w_down'].astype(u.dtype)
    x = x + rmsnorm(f, lw['norm_post_ffn'])
    return x, (s_new, dn_buf_new, fk, fv, logf, sk, sv, ffn_buf_new)


def setup_inputs(seed: int = 0) -> dict:
    key = jax.random.key(seed)
    ks = jax.random.split(key, 40)
    n_pages = PAST_LEN // PAGE_SIZE
    n_phys = (5 * DEC_BATCH * n_pages) // 4

    def nrm(k, shape, scale=1.0):
        return scale * jax.random.normal(k, shape, F32)

    def gain(k, shape):
        return 1.0 + nrm(k, shape, 0.02)

    x_prompt = nrm(ks[0], (BATCH, SEQ, D_MODEL))
    x_sample = nrm(ks[1], (DEC_BATCH, DEC_SEQ, D_MODEL))
    state_dn_S = nrm(ks[2], (DEPTH, DEC_BATCH, H_DN, HEAD_DIM, HEAD_DIM), HEAD_DIM ** -0.5)
    state_dn_conv = nrm(ks[3], (DEPTH, DEC_BATCH, DN_CONV - 1, 3 * DN_W))
    cache_fox_k = nrm(ks[4], (DEPTH, n_phys, PAGE_SIZE, H_FOX, HEAD_DIM))
    cache_fox_v = nrm(ks[5], (DEPTH, n_phys, PAGE_SIZE, H_FOX, HEAD_DIM))
    cache_fox_logf = jax.nn.log_sigmoid(1.0 + nrm(ks[6], (DEPTH, n_phys, PAGE_SIZE, H_FOX)))
    cache_sb_k = nrm(ks[7], (DEPTH, n_phys, PAGE_SIZE, H_SB, HEAD_DIM))
    cache_sb_v = nrm(ks[8], (DEPTH, n_phys, PAGE_SIZE, H_SB, HEAD_DIM))
    state_ffn_conv = nrm(ks[9], (DEPTH, DEC_BATCH, FFN_CONV - 1, 2 * D_FF))
    page_table = jax.random.permutation(ks[10], n_phys)[:DEC_BATCH * n_pages].reshape(DEC_BATCH, n_pages).astype(jnp.int32)

    norm_pre_mix = gain(ks[11], (DEPTH, D_MODEL))
    w_in = nrm(ks[12], (DEPTH, D_MODEL, IN_COLS), D_MODEL ** -0.5)
    dn_conv = nrm(ks[13], (DEPTH, DN_CONV, 3 * DN_W), DN_CONV ** -0.5)
    dn_A_log = jnp.log(jax.random.uniform(ks[14], (DEPTH, H_DN), F32, 1.0, 16.0))
    dt = jnp.exp(jax.random.uniform(ks[15], (DEPTH, H_DN), F32, float(np.log(1e-3)), float(np.log(1e-1))))
    dn_dt_bias = dt + jnp.log(-jnp.expm1(-dt))
    dn_norm = gain(ks[16], (DEPTH, HEAD_DIM))
    fox_b_f = 1.0 + nrm(ks[17], (DEPTH, H_FOX), 0.5)
    fox_norm = gain(ks[18], (DEPTH, HEAD_DIM))
    sb_norm = gain(ks[19], (DEPTH, HEAD_DIM))
    w_out = nrm(ks[20], (DEPTH, MIX_W, D_MODEL), MIX_W ** -0.5)
    norm_post_mix = gain(ks[21], (DEPTH, D_MODEL))
    norm_pre_ffn = gain(ks[22], (DEPTH, D_MODEL))
    ffn_w_up = nrm(ks[23], (DEPTH, D_MODEL, 2 * D_FF), D_MODEL ** -0.5)
    ffn_conv = nrm(ks[24], (DEPTH, FFN_CONV, 2 * D_FF), FFN_CONV ** -0.5)
    ffn_w_down = nrm(ks[25], (DEPTH, D_FF, D_MODEL), D_FF ** -0.5)
    norm_post_ffn = gain(ks[26], (DEPTH, D_MODEL))
    return {
        'x_prompt': x_prompt, 'x_sample': x_sample,
        'state_dn_S': state_dn_S, 'state_dn_conv': state_dn_conv,
        'cache_fox_k': cache_fox_k, 'cache_fox_v': cache_fox_v, 'cache_fox_logf': cache_fox_logf,
        'cache_sb_k': cache_sb_k, 'cache_sb_v': cache_sb_v,
        'state_ffn_conv': state_ffn_conv, 'page_table': page_table,
        'norm_pre_mix': norm_pre_mix, 'w_in': w_in, 'dn_conv': dn_conv,
        'dn_A_log': dn_A_log, 'dn_dt_bias': dn_dt_bias, 'dn_norm': dn_norm,
        'fox_b_f': fox_b_f, 'fox_norm': fox_norm, 'sb_norm': sb_norm,
        'w_out': w_out, 'norm_post_mix': norm_post_mix, 'norm_pre_ffn': norm_pre_ffn,
        'ffn_w_up': ffn_w_up, 'ffn_conv': ffn_conv, 'ffn_w_down': ffn_w_down,
        'norm_post_ffn': norm_post_ffn,
    }


def reference(x_prompt, x_sample, state_dn_S, state_dn_conv, cache_fox_k, cache_fox_v, cache_fox_logf,
              cache_sb_k, cache_sb_v, state_ffn_conv, page_table, norm_pre_mix, w_in, dn_conv,
              dn_A_log, dn_dt_bias, dn_norm, fox_b_f, fox_norm, sb_norm, w_out, norm_post_mix,
              norm_pre_ffn, ffn_w_up, ffn_conv, ffn_w_down, norm_post_ffn):
    yp, ys = x_prompt, x_sample
    bp = x_prompt.shape[0]
    prompt_new, sample_new = [], []
    for l in range(DEPTH):
        lw = {
            'norm_pre_mix': norm_pre_mix[l], 'w_in': w_in[l], 'dn_conv': dn_conv[l],
            'dn_A_log': dn_A_log[l], 'dn_dt_bias': dn_dt_bias[l], 'dn_norm': dn_norm[l],
            'fox_b_f': fox_b_f[l], 'fox_norm': fox_norm[l], 'sb_norm': sb_norm[l],
            'w_out': w_out[l], 'norm_post_mix': norm_post_mix[l], 'norm_pre_ffn': norm_pre_ffn[l],
            'ffn_w_up': ffn_w_up[l], 'ffn_conv': ffn_conv[l], 'ffn_w_down': ffn_w_down[l],
            'norm_post_ffn': norm_post_ffn[l],
        }
        s0 = jnp.zeros((bp, H_DN, HEAD_DIM, HEAD_DIM), F32)
        dn_buf0 = jnp.zeros((bp, DN_CONV - 1, 3 * DN_W), yp.dtype)
        ffn_buf0 = jnp.zeros((bp, FFN_CONV - 1, 2 * D_FF), yp.dtype)
        yp, st_p = trunk_layer(yp, lw, s0, dn_buf0, ffn_buf0, None, None)
        prompt_new.append(st_p)
        fox_past = (gather_pages(cache_fox_k[l], page_table), gather_pages(cache_fox_v[l], page_table),
                    gather_pages(cache_fox_logf[l], page_table))
        sb_past = (gather_pages(cache_sb_k[l], page_table), gather_pages(cache_sb_v[l], page_table))
        ys, st_s = trunk_layer(ys, lw, state_dn_S[l], state_dn_conv[l], state_ffn_conv[l], fox_past, sb_past)
        sample_new.append(st_s)
    (dn_S_p, dn_conv_p, fox_k_p, fox_v_p, fox_logf_p, sb_k_p, sb_v_p, ffn_conv_p) = [jnp.stack(a) for a in zip(*prompt_new)]
    (dn_S_s, dn_conv_s, fox_k_s, fox_v_s, fox_logf_s, sb_k_s, sb_v_s, ffn_conv_s) = [jnp.stack(a) for a in zip(*sample_new)]
    return (yp, ys,
            dn_S_p, dn_conv_p, fox_k_p, fox_v_p, fox_logf_p, sb_k_p, sb_v_p, ffn_conv_p,
            dn_S_s, dn_conv_s, fox_k_s, fox_v_s, fox_logf_s, sb_k_s, sb_v_s, ffn_conv_s)
```

```python
import functools

import jax
import jax.numpy as jnp
from jax import lax
from jax.experimental import pallas as pl
from jax.experimental.pallas import tpu as pltpu

F32 = jnp.float32
BF16 = jnp.bfloat16

HEAD_DIM = 128
H_DN = 6
H_FOX = 6
H_SB = 4
DN_W = H_DN * HEAD_DIM
FOX_W = H_FOX * HEAD_DIM
SB_W = H_SB * HEAD_DIM
DN_CONV = 4
FFN_CONV = 3
DN_CHUNK = 64
PAGE = 128
EPS = 1e-6
SCALE = HEAD_DIM ** -0.5
LANE = 128

BLK_QKV = 0
BLK_Z = 18
BLK_FQ = 24
BLK_FK = 30
BLK_FV = 36
BLK_GATE = 42
BLK_SQ = 44
BLK_SK = 48
BLK_SV = 52
N_BLK = 56
PACK_W = N_BLK * LANE

VMEM_LIMIT = 56 * 1024 * 1024
PAGES_PER_STEP = 8


def _cparams(*sem):
    return pltpu.CompilerParams(dimension_semantics=sem, vmem_limit_bytes=VMEM_LIMIT)


def _dot(a, b):
    return jnp.dot(a, b, preferred_element_type=F32)


def _dot_nt(a, b):
    return lax.dot_general(a, b, (((1,), (1,)), ((), ())), preferred_element_type=F32)


def _dot_tn(a, b):
    return lax.dot_general(a, b, (((0,), (0,)), ((), ())), preferred_element_type=F32)


def _dot_hi(a, b):
    return jnp.dot(a, b, preferred_element_type=F32, precision=lax.Precision.HIGHEST)


def _split(x):
    hi = x.astype(BF16)
    return hi, (x - hi.astype(F32)).astype(BF16)


def _dot3(a, b):
    return _dot(a[0], b[0]) + (_dot(a[0], b[1]) + _dot(a[1], b[0]))


def _sigmoid(x):
    return 1.0 / (1.0 + jnp.exp(-x))


def _silu(x):
    return x * _sigmoid(x)


def _neg_softplus(z):
    return -(jnp.maximum(z, 0.0) + jnp.log1p(jnp.exp(-jnp.abs(z))))


def _rms(x, w):
    return x * lax.rsqrt(jnp.mean(x * x, axis=-1, keepdims=True) + EPS) * w


def _head_slices(n):
    return [slice(h * HEAD_DIM, (h + 1) * HEAD_DIM) for h in range(n)]


def _norm_mm_kernel(x_ref, g_ref, w_ref, o_ref, hn_ref):
    @pl.when(pl.program_id(1) == 0)
    def _():
        hn_ref[...] = _rms(x_ref[...], g_ref[...]).astype(BF16)

    o_ref[...] = _dot(hn_ref[...], w_ref[...])


def _norm_matmul(x, g, w, *, tm, tn):
    m, d = x.shape
    n = w.shape[1]
    return pl.pallas_call(
        _norm_mm_kernel,
        out_shape=jax.ShapeDtypeStruct((m, n), F32),
        grid=(m // tm, n // tn),
        in_specs=[
            pl.BlockSpec((tm, d), lambda i, j: (i, 0)),
            pl.BlockSpec((1, d), lambda i, j: (0, 0)),
            pl.BlockSpec((d, tn), lambda i, j: (0, j)),
        ],
        out_specs=pl.BlockSpec((tm, tn), lambda i, j: (i, j)),
        scratch_shapes=[pltpu.VMEM((tm, d), BF16)],
        compiler_params=_cparams("parallel", "arbitrary"),
        name="norm_in_proj",
    )(x, g, w)


def _dn_pre_kernel(x_ref, halo_ref, st_ref, w_ref, q_ref, k_ref, v_ref, newst_ref, xc_ref, *, tt):
    t = pl.program_id(1)
    xc_ref[pl.ds(8, tt), :] = x_ref[0]

    @pl.when(t == 0)
    def _():
        xc_ref[5:8, :] = st_ref[0]

    @pl.when(t > 0)
    def _():
        xc_ref[0:8, :] = halo_ref[0]

    for part in range(3):
        for h in range(H_DN):
            cs = slice(part * DN_W + h * HEAD_DIM, part * DN_W + (h + 1) * HEAD_DIM)
            hs = slice(h * HEAD_DIM, (h + 1) * HEAD_DIM)
            y = xc_ref[pl.ds(5, tt), cs] * w_ref[0:1, cs]
            for i in range(1, DN_CONV):
                y = y + xc_ref[pl.ds(5 + i, tt), cs] * w_ref[i:i + 1, cs]
            y = _silu(y)
            if part == 2:
                v_ref[0, :, hs] = y
            else:
                yn = y * lax.rsqrt(jnp.sum(y * y, axis=-1, keepdims=True) + EPS)
                if part == 0:
                    q_ref[0, :, hs] = yn * SCALE
                else:
                    k_ref[0, :, hs] = yn
    newst_ref[0] = xc_ref[pl.ds(tt + 5, DN_CONV - 1), :]


def _dn_pre(proj3, state, w, *, tt):
    b, t, _ = proj3.shape
    c = 3 * DN_W
    hb = tt // 8
    out_shape = [jax.ShapeDtypeStruct((b, t, DN_W), F32)] * 3 + [jax.ShapeDtypeStruct((b, DN_CONV - 1, c), F32)]
    qkv_spec = pl.BlockSpec((1, tt, DN_W), lambda i, j: (i, j, 0))
    return pl.pallas_call(
        functools.partial(_dn_pre_kernel, tt=tt),
        out_shape=out_shape,
        grid=(b, t // tt),
        in_specs=[
            pl.BlockSpec((1, tt, c), lambda i, j: (i, j, 0)),
            pl.BlockSpec((1, 8, c), lambda i, j: (i, jnp.maximum(j * hb - 1, 0), 0)),
            pl.BlockSpec((1, DN_CONV - 1, c), lambda i, j: (i, 0, 0)),
            pl.BlockSpec((DN_CONV, c), lambda i, j: (0, 0)),
        ],
        out_specs=[qkv_spec, qkv_spec, qkv_spec, pl.BlockSpec((1, DN_CONV - 1, c), lambda i, j: (i, 0, 0))],
        scratch_shapes=[pltpu.VMEM((tt + 8, c), F32)],
        compiler_params=_cparams("parallel", "arbitrary"),
        name="dn_conv_prep",
    )(proj3, proj3, state, w)


def _scan_rows(x, row, limit):
    d = 1
    while d < limit:
        x = x + jnp.where(row >= d, pltpu.roll(x, d, axis=0), 0.0)
        d *= 2
    return x


def _gates_kernel(x_ref, bias_ref, alog_ref, gates_ref, cum_ref, gcl_ref, carry_ref, *, tt, chunk):
    t = pl.program_id(1)

    @pl.when(t == 0)
    def _():
        carry_ref[...] = jnp.zeros_like(carry_ref)

    x = x_ref[0] + bias_ref[...]
    lane = lax.broadcasted_iota(jnp.int32, x.shape, 1)
    row = lax.broadcasted_iota(jnp.int32, x.shape, 0)
    lp = jnp.log1p(jnp.exp(-jnp.abs(x)))
    softplus = jnp.maximum(x, 0.0) + lp
    logsig = jnp.minimum(x, 0.0) - lp
    decay = -jnp.exp(alog_ref[...]) * softplus
    out = jnp.where(lane < H_DN, _sigmoid(x),
                    jnp.where(lane < 2 * H_DN, decay,
                              jnp.where(lane < 2 * H_DN + H_FOX, logsig, 0.0)))
    gates_ref[0] = out
    c = _scan_rows(out, row, tt) + carry_ref[...]
    cum_ref[0] = c
    carry_ref[...] = c[tt - 1:tt, :]
    gcl_ref[0] = _scan_rows(out, row % chunk, min(chunk, tt))


def _gates(proj3, bias, alog, *, tt):
    b, t, _ = proj3.shape
    spec = pl.BlockSpec((1, tt, LANE), lambda i, j: (i, j, 0))
    return pl.pallas_call(
        functools.partial(_gates_kernel, tt=tt, chunk=DN_CHUNK),
        out_shape=[jax.ShapeDtypeStruct((b, t, LANE), F32)] * 3,
        grid=(b, t // tt),
        in_specs=[
            pl.BlockSpec((1, tt, LANE), lambda i, j: (i, j, BLK_GATE)),
            pl.BlockSpec((1, LANE), lambda i, j: (0, 0)),
            pl.BlockSpec((1, LANE), lambda i, j: (0, 0)),
        ],
        out_specs=[spec, spec, spec],
        scratch_shapes=[pltpu.VMEM((1, LANE), F32)],
        compiler_params=_cparams("parallel", "arbitrary"),
        name="gate_act",
    )(proj3, bias, alog)


def _gdn_kernel(q_ref, k_ref, v_ref, z_ref, gates_ref, gcl_ref, gclt_ref, s0_ref, nw_ref, o_ref, s_ref, *, c):
    @pl.when(pl.program_id(1) == 0)
    def _():
        s_ref[...] = s0_ref[...]

    ii = lax.broadcasted_iota(jnp.int32, (c, c), 0)
    jj = lax.broadcasted_iota(jnp.int32, (c, c), 1)
    lower = ii >= jj
    strict = ii > jj
    eye = jnp.where(ii == jj, 1.0, 0.0).astype(F32)
    n_sq = c.bit_length() - 2
    heads = range(H_DN)
    hsl = _head_slices(H_DN)
    q = [q_ref[0, :, hs] for hs in hsl]
    k = [k_ref[0, :, hs] for hs in hsl]
    v = [v_ref[0, :, hs] for hs in hsl]
    beta = [gates_ref[0, :, h:h + 1] for h in heads]
    gc = [gcl_ref[0, :, H_DN + h:H_DN + h + 1] for h in heads]
    gr = [gclt_ref[0, 0, h:h + 1, :] for h in heads]
    gl = [g[:, c - 1:c] for g in gr]
    gamma = [jnp.exp(jnp.where(lower, gc[h] - gr[h], -jnp.inf)) for h in heads]
    kb = [k[h] * beta[h] for h in heads]
    kbf = [x.astype(BF16) for x in k]
    kk = [_dot_nt(kb[h].astype(BF16), kbf[h]) for h in heads]
    mm = [jnp.where(strict, -(kk[h] * gamma[h]), 0.0) for h in heads]
    tinv = [eye + m for m in mm]
    ps = [_split(m) for m in mm]
    for _ in range(n_sq):
        ps = [_split(_dot3(x, x)) for x in ps]
        tinv = [tinv[h] + _dot3(_split(tinv[h]), ps[h]) for h in heads]
    egc = [jnp.exp(g) for g in gc]
    rhs = [_split(jnp.concatenate([kb[h] * egc[h], v[h] * beta[h]], axis=1)) for h in heads]
    wu = [_dot3(_split(tinv[h]), rhs[h]) for h in heads]
    qk = [_dot_nt(q[h].astype(BF16), kbf[h]) * gamma[h] for h in heads]
    s = [s_ref[0, h] for h in heads]
    sb = [x.astype(BF16) for x in s]
    v_new = [wu[h][:, HEAD_DIM:] - _dot(wu[h][:, :HEAD_DIM].astype(BF16), sb[h]) for h in heads]
    vnb = [x.astype(BF16) for x in v_new]
    for h in heads:
        k_dec = k[h] * jnp.exp(gl[h] - gc[h])
        s_ref[0, h] = s[h] * jnp.exp(gl[h]) + _dot_tn(k_dec.astype(BF16), vnb[h])
    for h in heads:
        o = _dot((q[h] * egc[h]).astype(BF16), sb[h]) + _dot(qk[h].astype(BF16), vnb[h])
        o_ref[0, :, hsl[h]] = (_rms(o, nw_ref[...]) * _silu(z_ref[0, :, hsl[h]])).astype(o_ref.dtype)


def _gdn(q, k, v, z, z_blk, gates, gcl, gclt, s0, nw):
    b, t, _ = q.shape
    c = DN_CHUNK
    blk = pl.BlockSpec((1, c, DN_W), lambda i, j: (i, j, 0))
    gblk = pl.BlockSpec((1, c, LANE), lambda i, j: (i, j, 0))
    sblk = pl.BlockSpec((1, H_DN, HEAD_DIM, HEAD_DIM), lambda i, j: (i, 0, 0, 0))
    return pl.pallas_call(
        functools.partial(_gdn_kernel, c=c),
        out_shape=[jax.ShapeDtypeStruct((b, t, DN_W), BF16), jax.ShapeDtypeStruct(s0.shape, F32)],
        grid=(b, t // c),
        in_specs=[
            blk, blk, blk,
            pl.BlockSpec((1, c, DN_W), lambda i, j: (i, j, z_blk)),
            gblk, gblk,
            pl.BlockSpec((1, 1, 8, c), lambda i, j: (i, j, 0, 0)),
            sblk,
            pl.BlockSpec((1, HEAD_DIM), lambda i, j: (0, 0)),
        ],
        out_specs=[blk, sblk],
        compiler_params=_cparams("parallel", "arbitrary"),
        name="gated_delta_rule",
    )(q, k, v, z, gates, gcl, gclt, s0, nw)


def _fox_kernel(q_ref, k_ref, v_ref, ct_ref, nw_ref, o_ref, qb_ref, m_ref, l_ref, acc_ref, *, tq):
    qi = pl.program_id(1)
    heads = range(H_FOX)
    hsl = _head_slices(H_FOX)
    row = lax.broadcasted_iota(jnp.int32, (tq, tq), 0)
    col = lax.broadcasted_iota(jnp.int32, (tq, tq), 1)
    qb_ref[...] = q_ref[0].astype(BF16)
    m_ref[...] = jnp.full_like(m_ref, -jnp.inf)
    l_ref[...] = jnp.zeros_like(l_ref)
    acc_ref[...] = jnp.zeros_like(acc_ref)

    def step(j, diagonal):
        ks = pl.ds(pl.multiple_of(j * tq, tq), tq)
        s = [_dot_nt(qb_ref[:, hsl[h]], k_ref[0, ks, hsl[h]].astype(BF16)) * SCALE - ct_ref[0, j, h:h + 1, :]
             for h in heads]
        if diagonal:
            s = [jnp.where(col <= row, x, -jnp.inf) for x in s]
        m_old = [m_ref[h] for h in heads]
        m_new = [jnp.maximum(m_old[h], jnp.max(s[h], axis=-1, keepdims=True)) for h in heads]
        p = [jnp.exp(s[h] - m_new[h]) for h in heads]
        a = [jnp.exp(m_old[h] - m_new[h]) for h in heads]
        for h in heads:
            m_ref[h] = m_new[h]
            l_ref[h] = a[h] * l_ref[h] + jnp.sum(p[h], axis=-1, keepdims=True)
            acc_ref[h] = a[h] * acc_ref[h] + _dot(p[h].astype(BF16), v_ref[0, ks, hsl[h]].astype(BF16))

    def body(j, carry):
        step(j, False)
        return carry

    lax.fori_loop(0, qi, body, 0)
    step(qi, True)
    for h in heads:
        o_ref[0, :, hsl[h]] = _rms(acc_ref[h] / l_ref[h], nw_ref[...]).astype(o_ref.dtype)


def _fox_prompt(proj3, ct, nw, *, tq):
    b, t, _ = proj3.shape
    return pl.pallas_call(
        functools.partial(_fox_kernel, tq=tq),
        out_shape=jax.ShapeDtypeStruct((b, t, FOX_W), BF16),
        grid=(b, t // tq),
        in_specs=[
            pl.BlockSpec((1, tq, FOX_W), lambda i, j: (i, j, BLK_FQ // H_FOX)),
            pl.BlockSpec((1, t, FOX_W), lambda i, j: (i, 0, BLK_FK // H_FOX)),
            pl.BlockSpec((1, t, FOX_W), lambda i, j: (i, 0, BLK_FV // H_FOX)),
            pl.BlockSpec((1, t // tq, 8, tq), lambda i, j: (i, 0, 0, 0)),
            pl.BlockSpec((1, HEAD_DIM), lambda i, j: (0, 0)),
        ],
        out_specs=pl.BlockSpec((1, tq, FOX_W), lambda i, j: (i, j, 0)),
        scratch_shapes=[
            pltpu.VMEM((tq, FOX_W), BF16),
            pltpu.VMEM((H_FOX, tq, 1), F32),
            pltpu.VMEM((H_FOX, tq, 1), F32),
            pltpu.VMEM((H_FOX, tq, HEAD_DIM), F32),
        ],
        compiler_params=_cparams("parallel", "arbitrary"),
        name="fox_prompt",
    )(proj3, proj3, proj3, ct, nw)


def _suffix_incl(lg, tri):
    hi, lo = _split(lg)
    return _dot(hi, tri) + _dot(lo, tri)


def _sb_kernel(q_ref, k_ref, v_ref, nw_ref, o_ref, qb_ref, r_ref, acc_ref, *, tq):
    qi = pl.program_id(1)
    heads = range(H_SB)
    hsl = _head_slices(H_SB)
    row = lax.broadcasted_iota(jnp.int32, (tq, tq), 0)
    col = lax.broadcasted_iota(jnp.int32, (tq, tq), 1)
    tri = jnp.where(row >= col, 1.0, 0.0).astype(BF16)
    qb_ref[...] = q_ref[0].astype(BF16)
    r_ref[...] = jnp.zeros_like(r_ref)
    acc_ref[...] = jnp.zeros_like(acc_ref)

    def step(jr, diagonal):
        ks = pl.ds(pl.multiple_of((qi - jr) * tq, tq), tq)
        z = [_dot_nt(qb_ref[:, hsl[h]], k_ref[0, ks, hsl[h]].astype(BF16)) * SCALE for h in heads]
        lg = [_neg_softplus(x) for x in z]
        if diagonal:
            lg = [jnp.where(col < row, x, 0.0) for x in lg]
        incl = [_suffix_incl(x, tri) for x in lg]
        a = [jnp.exp(z[h] + incl[h] + r_ref[h]) for h in heads]
        if diagonal:
            a = [jnp.where(col < row, x, 0.0) for x in a]
        for h in heads:
            acc_ref[h] += _dot(a[h].astype(BF16), v_ref[0, ks, hsl[h]].astype(BF16))
            r_ref[h] += incl[h][:, 0:1]

    def body(jr, carry):
        step(jr, False)
        return carry

    step(0, True)
    lax.fori_loop(1, qi + 1, body, 0)
    for h in heads:
        o_ref[0, :, hsl[h]] = _rms(acc_ref[h], nw_ref[...]).astype(o_ref.dtype)


def _sb_prompt(proj3, nw, *, tq):
    b, t, _ = proj3.shape
    return pl.pallas_call(
        functools.partial(_sb_kernel, tq=tq),
        out_shape=jax.ShapeDtypeStruct((b, t, SB_W), BF16),
        grid=(b, t // tq),
        in_specs=[
            pl.BlockSpec((1, tq, SB_W), lambda i, j: (i, j, BLK_SQ // H_SB)),
            pl.BlockSpec((1, t, SB_W), lambda i, j: (i, 0, BLK_SK // H_SB)),
            pl.BlockSpec((1, t, SB_W), lambda i, j: (i, 0, BLK_SV // H_SB)),
            pl.BlockSpec((1, HEAD_DIM), lambda i, j: (0, 0)),
        ],
        out_specs=pl.BlockSpec((1, tq, SB_W), lambda i, j: (i, j, 0)),
        scratch_shapes=[
            pltpu.VMEM((tq, SB_W), BF16),
            pltpu.VMEM((H_SB, tq, 1), F32),
            pltpu.VMEM((H_SB, tq, HEAD_DIM), F32),
        ],
        compiler_params=_cparams("parallel", "arbitrary"),
        name="sb_prompt",
    )(proj3, proj3, proj3, nw)


def _page_rows(page_ref):
    x = jnp.swapaxes(page_ref[...], 0, 1)
    return jnp.concatenate([x[h] for h in range(x.shape[0])], axis=1).astype(BF16)


def _block_diag_q(q_ref, nh, t):
    zero = jnp.zeros((t, HEAD_DIM), F32)
    rows = [jnp.concatenate([q_ref[0, :, h * HEAD_DIM:(h + 1) * HEAD_DIM] if j == h else zero
                             for j in range(nh)], axis=1) for h in range(nh)]
    return jnp.concatenate(rows, axis=0).astype(BF16)


def _rows_per_head(x, nh, t):
    return jnp.concatenate([jnp.broadcast_to(x[h:h + 1, :], (t, x.shape[1])) for h in range(nh)], axis=0)


def _scores(qblk, keys):
    return _dot_nt(qblk, keys) * SCALE


def _weighted_values(w, values, nh, t):
    full = _dot(w.astype(BF16), values)
    return jnp.concatenate([full[h * t:(h + 1) * t, h * HEAD_DIM:(h + 1) * HEAD_DIM] for h in range(nh)], axis=0)


def _fox_dec_kernel(pt_ref, q_ref, kn_ref, vn_ref, cn_ref, nw_ref, *rest, g, t):
    kp, vp, lp = rest[:g], rest[g:2 * g], rest[2 * g:3 * g]
    o_ref = rest[3 * g]
    m_ref, l_ref, acc_ref, off_ref, kpad_ref, vpad_ref = rest[3 * g + 1:]
    nh = H_FOX
    hsl = _head_slices(nh)
    p = pl.program_id(1)

    @pl.when(p == 0)
    def _():
        m_ref[...] = jnp.full_like(m_ref, -jnp.inf)
        l_ref[...] = jnp.zeros_like(l_ref)
        acc_ref[...] = jnp.zeros_like(acc_ref)
        off_ref[...] = jnp.zeros_like(off_ref)

    qblk = _block_diag_q(q_ref, nh, t)
    ii = lax.broadcasted_iota(jnp.int32, (PAGE, PAGE), 0)
    jj = lax.broadcasted_iota(jnp.int32, (PAGE, PAGE), 1)
    upper = jnp.where(ii <= jj, 1.0, 0.0).astype(F32)

    def update(s, weighted_values):
        m = m_ref[...]
        m_new = jnp.maximum(m, jnp.max(s, axis=-1, keepdims=True))
        pr = jnp.exp(s - m_new)
        a = jnp.exp(m - m_new)
        l_ref[...] = a * l_ref[...] + jnp.sum(pr, axis=-1, keepdims=True)
        acc_ref[...] = a * acc_ref[...] + weighted_values(pr)
        m_ref[...] = m_new

    cw = [_dot_hi(lp[i][...], upper) for i in range(g)]
    off = off_ref[...]
    s_pages = []
    for i in range(g):
        s_pages.append(_scores(qblk, _page_rows(kp[i])) - _rows_per_head(off + cw[i], nh, t))
        off = off + cw[i][:, PAGE - 1:PAGE]
    off_ref[...] = off
    values = jnp.concatenate([_page_rows(vp[i]) for i in range(g)], axis=0)
    update(jnp.concatenate(s_pages, axis=1), lambda pr: _weighted_values(pr, values, nh, t))

    @pl.when(p == pl.num_programs(1) - 1)
    def _():
        kpad_ref[...] = jnp.zeros_like(kpad_ref)
        vpad_ref[...] = jnp.zeros_like(vpad_ref)
        kpad_ref[0:t, :] = kn_ref[0]
        vpad_ref[0:t, :] = vn_ref[0]
        c = off_ref[...] + cn_ref[0]
        s = _scores(qblk, kpad_ref[...].astype(BF16)) - _rows_per_head(c, nh, t)
        r = lax.broadcasted_iota(jnp.int32, s.shape, 0)
        cc = lax.broadcasted_iota(jnp.int32, s.shape, 1)
        s = jnp.where(cc <= r % t, s, -jnp.inf)
        update(s, lambda pr: _weighted_values(pr, vpad_ref[...].astype(BF16), nh, t))
        o = acc_ref[...] / l_ref[...]
        for h in range(nh):
            o_ref[0, :, hsl[h]] = _rms(o[h * t:(h + 1) * t], nw_ref[...]).astype(o_ref.dtype)


def _fox_dec(page_table, proj3, cnt, nw, k_pool, v_pool, lf_pool, layer, *, g):
    b, t, _ = proj3.shape
    npg = page_table.shape[1]
    nh = H_FOX

    def page_spec(shape, i):
        return pl.BlockSpec((None, None) + shape, lambda bi, p, pt, i=i: (layer, pt[bi, p * g + i], 0, 0, 0)[:2 + len(shape)])

    in_specs = [
        pl.BlockSpec((1, t, FOX_W), lambda bi, p, pt: (bi, 0, BLK_FQ // nh)),
        pl.BlockSpec((1, t, FOX_W), lambda bi, p, pt: (bi, 0, BLK_FK // nh)),
        pl.BlockSpec((1, t, FOX_W), lambda bi, p, pt: (bi, 0, BLK_FV // nh)),
        pl.BlockSpec((1, 8, PAGE), lambda bi, p, pt: (bi, 0, 0)),
        pl.BlockSpec((1, HEAD_DIM), lambda bi, p, pt: (0, 0)),
    ]
    in_specs += [page_spec((PAGE, nh, HEAD_DIM), i) for i in range(g)]
    in_specs += [page_spec((PAGE, nh, HEAD_DIM), i) for i in range(g)]
    in_specs += [page_spec((8, PAGE), i) for i in range(g)]
    grid_spec = pltpu.PrefetchScalarGridSpec(
        num_scalar_prefetch=1,
        grid=(b, npg // g),
        in_specs=in_specs,
        out_specs=pl.BlockSpec((1, t, FOX_W), lambda bi, p, pt: (bi, 0, 0)),
        scratch_shapes=[
            pltpu.VMEM((nh * t, 1), F32),
            pltpu.VMEM((nh * t, 1), F32),
            pltpu.VMEM((nh * t, HEAD_DIM), F32),
            pltpu.VMEM((8, 1), F32),
            pltpu.VMEM((PAGE, FOX_W), F32),
            pltpu.VMEM((PAGE, FOX_W), F32),
        ],
    )
    return pl.pallas_call(
        functools.partial(_fox_dec_kernel, g=g, t=t),
        out_shape=jax.ShapeDtypeStruct((b, t, FOX_W), BF16),
        grid_spec=grid_spec,
        compiler_params=_cparams("parallel", "arbitrary"),
        name="fox_paged",
    )(page_table, proj3, proj3, proj3, cnt, nw, *([k_pool] * g), *([v_pool] * g), *([lf_pool] * g))


def _sb_dec_kernel(pt_ref, q_ref, kn_ref, vn_ref, nw_ref, *rest, g, t):
    kp, vp = rest[:g], rest[g:2 * g]
    o_ref = rest[2 * g]
    rsum_ref, acc_ref, kpad_ref, vpad_ref = rest[2 * g + 1:]
    nh = H_SB
    hsl = _head_slices(nh)
    p = pl.program_id(1)
    ii = lax.broadcasted_iota(jnp.int32, (PAGE, PAGE), 0)
    jj = lax.broadcasted_iota(jnp.int32, (PAGE, PAGE), 1)
    tri = jnp.where(ii >= jj, 1.0, 0.0).astype(BF16)
    qblk = _block_diag_q(q_ref, nh, t)

    @pl.when(p == 0)
    def _():
        kpad_ref[...] = jnp.zeros_like(kpad_ref)
        vpad_ref[...] = jnp.zeros_like(vpad_ref)
        kpad_ref[0:t, :] = kn_ref[0]
        vpad_ref[0:t, :] = vn_ref[0]
        z = _scores(qblk, kpad_ref[...].astype(BF16))
        r = lax.broadcasted_iota(jnp.int32, z.shape, 0)
        cc = lax.broadcasted_iota(jnp.int32, z.shape, 1)
        valid = cc < r % t
        incl = _suffix_incl(jnp.where(valid, _neg_softplus(z), 0.0), tri)
        a = jnp.where(valid, jnp.exp(z + incl), 0.0)
        rsum_ref[...] = incl[:, 0:1]
        acc_ref[...] = _weighted_values(a, vpad_ref[...].astype(BF16), nh, t)

    zs, incls = [], []
    for i in range(g):
        z = _scores(qblk, _page_rows(kp[i]))
        zs.append(z)
        incls.append(_suffix_incl(_neg_softplus(z), tri))
    rsum = rsum_ref[...]
    weights = []
    for i in range(g):
        weights.append(jnp.exp(zs[i] + incls[i] + rsum))
        rsum = rsum + incls[i][:, 0:1]
    rsum_ref[...] = rsum
    values = jnp.concatenate([_page_rows(vp[i]) for i in range(g)], axis=0)
    acc_ref[...] += _weighted_values(jnp.concatenate(weights, axis=1), values, nh, t)

    @pl.when(p == pl.num_programs(1) - 1)
    def _():
        acc = acc_ref[...]
        for h in range(nh):
            o_ref[0, :, hsl[h]] = _rms(acc[h * t:(h + 1) * t], nw_ref[...]).astype(o_ref.dtype)


def _sb_dec(page_table, proj3, nw, k_pool, v_pool, layer, *, g):
    b, t, _ = proj3.shape
    npg = page_table.shape[1]
    nh = H_SB

    def page_spec(i):
        return pl.BlockSpec((None, None, PAGE, nh, HEAD_DIM),
                            lambda bi, p, pt, i=i: (layer, pt[bi, npg - 1 - (p * g + i)], 0, 0, 0))

    in_specs = [
        pl.BlockSpec((1, t, SB_W), lambda bi, p, pt: (bi, 0, BLK_SQ // nh)),
        pl.BlockSpec((1, t, SB_W), lambda bi, p, pt: (bi, 0, BLK_SK // nh)),
        pl.BlockSpec((1, t, SB_W), lambda bi, p, pt: (bi, 0, BLK_SV // nh)),
        pl.BlockSpec((1, HEAD_DIM), lambda bi, p, pt: (0, 0)),
    ]
    in_specs += [page_spec(i) for i in range(g)] * 2
    grid_spec = pltpu.PrefetchScalarGridSpec(
        num_scalar_prefetch=1,
        grid=(b, npg // g),
        in_specs=in_specs,
        out_specs=pl.BlockSpec((1, t, SB_W), lambda bi, p, pt: (bi, 0, 0)),
        scratch_shapes=[
            pltpu.VMEM((nh * t, 1), F32),
            pltpu.VMEM((nh * t, HEAD_DIM), F32),
            pltpu.VMEM((PAGE, SB_W), F32),
            pltpu.VMEM((PAGE, SB_W), F32),
        ],
    )
    return pl.pallas_call(
        functools.partial(_sb_dec_kernel, g=g, t=t),
        out_shape=jax.ShapeDtypeStruct((b, t, SB_W), BF16),
        grid_spec=grid_spec,
        compiler_params=_cparams("parallel", "arbitrary"),
        name="sb_paged",
    )(page_table, proj3, proj3, proj3, nw, *([k_pool] * g), *([v_pool] * g))


def _out_proj_kernel(x_ref, a_ref, b_ref, c_ref, w_ref, g_ref, o_ref):
    acc = _dot(a_ref[...], w_ref[0:DN_W, :])
    acc += _dot(b_ref[...], w_ref[DN_W:DN_W + FOX_W, :])
    acc += _dot(c_ref[...], w_ref[DN_W + FOX_W:, :])
    o_ref[...] = x_ref[...] + _rms(acc, g_ref[...])


def _out_proj(x, oa, ob, oc, w, g, *, tm):
    m, d = x.shape
    return pl.pallas_call(
        _out_proj_kernel,
        out_shape=jax.ShapeDtypeStruct((m, d), F32),
        grid=(m // tm,),
        in_specs=[
            pl.BlockSpec((tm, d), lambda i: (i, 0)),
            pl.BlockSpec((tm, DN_W), lambda i: (i, 0)),
            pl.BlockSpec((tm, FOX_W), lambda i: (i, 0)),
            pl.BlockSpec((tm, SB_W), lambda i: (i, 0)),
            pl.BlockSpec(w.shape, lambda i: (0, 0)),
            pl.BlockSpec((1, d), lambda i: (0, 0)),
        ],
        out_specs=pl.BlockSpec((tm, d), lambda i: (i, 0)),
        compiler_params=_cparams("parallel"),
        name="out_proj",
    )(x, oa, ob, oc, w, g)


def _ffn_tail(c, x_ref, gpost_ref, acc_ref, o_ref):
    @pl.when(c == pl.num_programs(1) - 1)
    def _():
        o_ref[...] = x_ref[...] + _rms(acc_ref[...], gpost_ref[...])


def _ffn_prompt_kernel(x_ref, halo_ref, gpre_ref, wg_ref, wu_ref, cg_ref, cu_ref, wd_ref, gpost_ref,
                       o_ref, st_ref, hn_ref, acc_ref, ug_ref, uu_ref, *, tm, tiles_per_seq):
    i = pl.program_id(0)
    c = pl.program_id(1)

    @pl.when(c == 0)
    def _():
        hn_ref[pl.ds(8, tm), :] = _rms(x_ref[...], gpre_ref[...]).astype(BF16)
        keep = jnp.where(i % tiles_per_seq == 0, 0.0, 1.0)
        hn_ref[0:8, :] = (_rms(halo_ref[...], gpre_ref[...]) * keep).astype(BF16)
        acc_ref[...] = jnp.zeros_like(acc_ref)

    hn = hn_ref[...]
    ug_ref[...] = _dot(hn, wg_ref[...])
    uu_ref[...] = _dot(hn, wu_ref[...])

    def conv(u_ref, w_ref):
        y = u_ref[pl.ds(8 - (FFN_CONV - 1), tm), :] * w_ref[0:1, :]
        for k in range(1, FFN_CONV):
            y = y + u_ref[pl.ds(8 - (FFN_CONV - 1) + k, tm), :] * w_ref[k:k + 1, :]
        return y

    act = _silu(conv(ug_ref, cg_ref)) * conv(uu_ref, cu_ref)
    acc_ref[...] += _dot(act.astype(BF16), wd_ref[...])
    st_ref[0, 0] = ug_ref[pl.ds(tm + 8 - (FFN_CONV - 1), FFN_CONV - 1), :]
    st_ref[0, 1] = uu_ref[pl.ds(tm + 8 - (FFN_CONV - 1), FFN_CONV - 1), :]
    _ffn_tail(c, x_ref, gpost_ref, acc_ref, o_ref)


def _ffn_prompt(x, seq_len, gpre, w_up, conv_w, w_down, gpost, *, tm, tc):
    m, d = x.shape
    dff = w_down.shape[0]
    nc = dff // tc
    tps = seq_len // tm
    hb = tm // 8
    return pl.pallas_call(
        functools.partial(_ffn_prompt_kernel, tm=tm, tiles_per_seq=tps),
        out_shape=[jax.ShapeDtypeStruct((m, d), F32),
                   jax.ShapeDtypeStruct((m // tm, 2, FFN_CONV - 1, dff), F32)],
        grid=(m // tm, nc),
        in_specs=[
            pl.BlockSpec((tm, d), lambda i, c: (i, 0)),
            pl.BlockSpec((8, d), lambda i, c: (jnp.maximum(i * hb - 1, 0), 0)),
            pl.BlockSpec((1, d), lambda i, c: (0, 0)),
            pl.BlockSpec((d, tc), lambda i, c: (0, c)),
            pl.BlockSpec((d, tc), lambda i, c: (0, nc + c)),
            pl.BlockSpec((FFN_CONV, tc), lambda i, c: (0, c)),
            pl.BlockSpec((FFN_CONV, tc), lambda i, c: (0, nc + c)),
            pl.BlockSpec((tc, d), lambda i, c: (c, 0)),
            pl.BlockSpec((1, d), lambda i, c: (0, 0)),
        ],
        out_specs=[
            pl.BlockSpec((tm, d), lambda i, c: (i, 0)),
            pl.BlockSpec((1, 2, FFN_CONV - 1, tc), lambda i, c: (i, 0, 0, c)),
        ],
        scratch_shapes=[
            pltpu.VMEM((tm + 8, d), BF16),
            pltpu.VMEM((tm, d), F32),
            pltpu.VMEM((tm + 8, tc), F32),
            pltpu.VMEM((tm + 8, tc), F32),
        ],
        compiler_params=_cparams("parallel", "arbitrary"),
        name="conv_ffn_prompt",
    )(x, x, gpre, w_up, w_up, conv_w, conv_w, w_down, gpost)


def _ffn_dec_kernel(x_ref, gpre_ref, wg_ref, wu_ref, cg_ref, cu_ref, wd_ref, gpost_ref,
                    s0g_ref, s1g_ref, s0u_ref, s1u_ref,
                    o_ref, ugo_ref, uuo_ref, hn_ref, acc_ref, ug_ref, uu_ref, *, tm, t):
    c = pl.program_id(1)

    @pl.when(c == 0)
    def _():
        hn_ref[...] = _rms(x_ref[...], gpre_ref[...]).astype(BF16)
        acc_ref[...] = jnp.zeros_like(acc_ref)
        ug_ref[0:8, :] = jnp.zeros((8, ug_ref.shape[1]), F32)
        uu_ref[0:8, :] = jnp.zeros((8, uu_ref.shape[1]), F32)

    hn = hn_ref[...]
    ug = _dot(hn, wg_ref[...])
    uu = _dot(hn, wu_ref[...])
    ugo_ref[...] = ug
    uuo_ref[...] = uu
    ug_ref[pl.ds(8, tm), :] = ug
    uu_ref[pl.ds(8, tm), :] = uu
    pos = lax.broadcasted_iota(jnp.int32, ug.shape, 0) % t

    def conv(u, u_ref, s0_ref, s1_ref, w_ref):
        um1 = jnp.where(pos >= 1, u_ref[pl.ds(7, tm), :], s1_ref[...])
        um2 = jnp.where(pos >= 2, u_ref[pl.ds(6, tm), :], jnp.where(pos == 0, s0_ref[...], s1_ref[...]))
        return um2 * w_ref[0:1, :] + um1 * w_ref[1:2, :] + u * w_ref[2:3, :]

    act = _silu(conv(ug, ug_ref, s0g_ref, s1g_ref, cg_ref)) * conv(uu, uu_ref, s0u_ref, s1u_ref, cu_ref)
    acc_ref[...] += _dot(act.astype(BF16), wd_ref[...])
    _ffn_tail(c, x_ref, gpost_ref, acc_ref, o_ref)


def _ffn_dec(x, t, gpre, w_up, conv_w, w_down, gpost, s0e, s1e, *, tc):
    m, d = x.shape
    dff = w_down.shape[0]
    nc = dff // tc
    tm = m
    st = lambda off: pl.BlockSpec((tm, tc), lambda i, c, off=off: (0, off * nc + c))
    return pl.pallas_call(
        functools.partial(_ffn_dec_kernel, tm=tm, t=t),
        out_shape=[jax.ShapeDtypeStruct((m, d), F32),
                   jax.ShapeDtypeStruct((m, dff), F32), jax.ShapeDtypeStruct((m, dff), F32)],
        grid=(1, nc),
        in_specs=[
            pl.BlockSpec((tm, d), lambda i, c: (0, 0)),
            pl.BlockSpec((1, d), lambda i, c: (0, 0)),
            pl.BlockSpec((d, tc), lambda i, c: (0, c)),
            pl.BlockSpec((d, tc), lambda i, c: (0, nc + c)),
            pl.BlockSpec((FFN_CONV, tc), lambda i, c: (0, c)),
            pl.BlockSpec((FFN_CONV, tc), lambda i, c: (0, nc + c)),
            pl.BlockSpec((tc, d), lambda i, c: (c, 0)),
            pl.BlockSpec((1, d), lambda i, c: (0, 0)),
            st(0), st(0), st(1), st(1),
        ],
        out_specs=[
            pl.BlockSpec((tm, d), lambda i, c: (0, 0)),
            pl.BlockSpec((tm, tc), lambda i, c: (0, c)),
            pl.BlockSpec((tm, tc), lambda i, c: (0, c)),
        ],
        scratch_shapes=[
            pltpu.VMEM((tm, d), BF16),
            pltpu.VMEM((tm, d), F32),
            pltpu.VMEM((tm + 8, tc), F32),
            pltpu.VMEM((tm + 8, tc), F32),
        ],
        compiler_params=_cparams("arbitrary", "arbitrary"),
        name="conv_ffn_sample",
    )(x, gpre, w_up, w_up, conv_w, conv_w, w_down, gpost, s0e, s1e, s0e, s1e)


def _pack_w_in(w_in):
    d = w_in.shape[0]
    sizes = (3 * DN_W, H_DN, H_DN, DN_W, FOX_W, FOX_W, FOX_W, H_FOX, SB_W, SB_W, SB_W)
    parts, start = [], 0
    for s in sizes:
        parts.append(w_in[:, start:start + s])
        start += s
    qkv, dn_b, dn_a, dn_z, fq, fk, fv, ff, sq, sk, sv = parts
    gate = jnp.concatenate([dn_b, dn_a, ff, jnp.zeros((d, 2 * LANE - 2 * H_DN - H_FOX), w_in.dtype)], axis=1)
    return jnp.concatenate([qkv, dn_z, fq, fk, fv, gate, sq, sk, sv], axis=1).astype(BF16)


def _lane_row(pairs):
    row = jnp.zeros((LANE,), F32)
    for off, v in pairs:
        row = row.at[off:off + v.shape[0]].set(v.astype(F32))
    return row[None, :]


def _mixers_common(x3, lw, dn_state, tm, tt):
    b, t, d = x3.shape
    proj = _norm_matmul(x3.reshape(b * t, d), lw["norm_pre_mix"], lw["w_in"], tm=tm, tn=512)
    proj3 = proj.reshape(b, t, PACK_W)
    q, k, v, dn_buf_new = _dn_pre(proj3, dn_state, lw["dn_conv"], tt=tt)
    gates, cum, gcl = _gates(proj3, lw["gate_bias"], lw["alog_row"], tt=tt)
    return proj3, q, k, v, dn_buf_new, gates, cum, gcl


def _gclt(gcl, c):
    b, t, _ = gcl.shape
    g = gcl[:, :, H_DN:2 * H_DN].reshape(b, t // c, c, H_DN)
    g = jnp.swapaxes(g, 2, 3)
    return jnp.pad(g, ((0, 0), (0, 0), (0, 8 - H_DN), (0, 0)))


def _new_rows(proj3, gates):
    b, t, _ = proj3.shape

    def cols(blk, w, nh):
        return proj3[:, :, blk * LANE:blk * LANE + w].reshape(b, t, nh, HEAD_DIM)

    return (cols(BLK_FK, FOX_W, H_FOX), cols(BLK_FV, FOX_W, H_FOX), gates[:, :, 2 * H_DN:2 * H_DN + H_FOX],
            cols(BLK_SK, SB_W, H_SB), cols(BLK_SV, SB_W, H_SB))


def _layer_prompt(x3, lw):
    b, t, d = x3.shape
    tq = 256
    tm_ffn = 512
    zeros_dn = jnp.zeros((b, DN_CONV - 1, 3 * DN_W), F32)
    proj3, q, k, v, dn_buf_new, gates, cum, gcl = _mixers_common(x3, lw, zeros_dn, tm=1024, tt=256)
    s0 = jnp.zeros((b, H_DN, HEAD_DIM, HEAD_DIM), F32)
    o_a, s_new = _gdn(q, k, v, proj3, BLK_Z // H_DN, gates, gcl, _gclt(gcl, DN_CHUNK), s0, lw["dn_norm"])
    ct = cum[:, :, 2 * H_DN:2 * H_DN + H_FOX].reshape(b, t // tq, tq, H_FOX)
    ct = jnp.pad(jnp.swapaxes(ct, 2, 3), ((0, 0), (0, 0), (0, 8 - H_FOX), (0, 0)))
    o_b = _fox_prompt(proj3, ct, lw["fox_norm"], tq=tq)
    o_c = _sb_prompt(proj3, lw["sb_norm"], tq=tq)
    m = b * t
    x1 = _out_proj(x3.reshape(m, d), o_a.reshape(m, DN_W), o_b.reshape(m, FOX_W), o_c.reshape(m, SB_W),
                   lw["w_out"], lw["norm_post_mix"], tm=512)
    y, st = _ffn_prompt(x1, t, lw["norm_pre_ffn"], lw["ffn_w_up"], lw["ffn_conv"], lw["ffn_w_down"],
                        lw["norm_post_ffn"], tm=tm_ffn, tc=512)
    tps = t // tm_ffn
    ffn_buf_new = jnp.swapaxes(st[tps - 1::tps], 1, 2).reshape(b, FFN_CONV - 1, -1)
    fk, fv, logf, sk, sv = _new_rows(proj3, gates)
    return y.reshape(b, t, d), (s_new, dn_buf_new, fk, fv, logf, sk, sv, ffn_buf_new)


def _layer_sample(x3, lw, dn_s0, dn_buf, ffn_buf, pools, page_table, layer):
    b, t, d = x3.shape
    c = DN_CHUNK
    fox_k, fox_v, fox_lf, sb_k, sb_v = pools
    proj3, q, k, v, dn_buf_new, gates, cum, gcl = _mixers_common(x3, lw, dn_buf, tm=b * t, tt=t)
    padt = ((0, 0), (0, c - t), (0, 0))
    z = proj3[:, :, BLK_Z * LANE:BLK_Z * LANE + DN_W]
    gcl_pad = jnp.pad(gcl, padt, mode="edge")
    o_a, s_new = _gdn(jnp.pad(q, padt), jnp.pad(k, padt), jnp.pad(v, padt), jnp.pad(z, padt), 0,
                      jnp.pad(gates, padt), gcl_pad, _gclt(gcl_pad, c), dn_s0, lw["dn_norm"])
    o_a = o_a[:, :t]
    cnt = jnp.swapaxes(cum[:, :, 2 * H_DN:2 * H_DN + H_FOX], 1, 2)
    cnt = jnp.pad(cnt, ((0, 0), (0, 8 - H_FOX), (0, PAGE - t)))
    o_b = _fox_dec(page_table, proj3, cnt, lw["fox_norm"], fox_k, fox_v, fox_lf, layer, g=PAGES_PER_STEP)
    o_c = _sb_dec(page_table, proj3, lw["sb_norm"], sb_k, sb_v, layer, g=PAGES_PER_STEP)
    m = b * t
    x1 = _out_proj(x3.reshape(m, d), o_a.reshape(m, DN_W), o_b.reshape(m, FOX_W), o_c.reshape(m, SB_W),
                   lw["w_out"], lw["norm_post_mix"], tm=m)
    s0e = jnp.repeat(ffn_buf[:, 0, :], t, axis=0)
    s1e = jnp.repeat(ffn_buf[:, 1, :], t, axis=0)
    y, ug, uu = _ffn_dec(x1, t, lw["norm_pre_ffn"], lw["ffn_w_up"], lw["ffn_conv"], lw["ffn_w_down"],
                         lw["norm_post_ffn"], s0e, s1e, tc=512)
    u = jnp.concatenate([ug, uu], axis=1).reshape(b, t, -1)
    ffn_buf_new = u[:, t - (FFN_CONV - 1):]
    fk, fv, logf, sk, sv = _new_rows(proj3, gates)
    return y.reshape(b, t, d), (s_new, dn_buf_new, fk, fv, logf, sk, sv, ffn_buf_new)


def kernel(x_prompt, x_sample, state_dn_S, state_dn_conv, cache_fox_k, cache_fox_v, cache_fox_logf, cache_sb_k, cache_sb_v, state_ffn_conv, page_table, norm_pre_mix, w_in, dn_conv, dn_A_log, dn_dt_bias, dn_norm, fox_b_f, fox_norm, sb_norm, w_out, norm_post_mix, norm_pre_ffn, ffn_w_up, ffn_conv, ffn_w_down, norm_post_ffn):
    depth = w_in.shape[0]
    yp, ys = x_prompt, x_sample
    prompt_new, sample_new = [], []
    lf_t = jnp.pad(jnp.swapaxes(cache_fox_logf, 2, 3), ((0, 0), (0, 0), (0, 8 - H_FOX), (0, 0)))
    pools = (cache_fox_k, cache_fox_v, lf_t, cache_sb_k, cache_sb_v)
    for l in range(depth):
        lw = {
            "norm_pre_mix": norm_pre_mix[l][None, :],
            "w_in": _pack_w_in(w_in[l]),
            "dn_conv": dn_conv[l],
            "gate_bias": _lane_row([(H_DN, dn_dt_bias[l]), (2 * H_DN, fox_b_f[l])]),
            "alog_row": _lane_row([(H_DN, dn_A_log[l])]),
            "dn_norm": dn_norm[l][None, :],
            "fox_norm": fox_norm[l][None, :],
            "sb_norm": sb_norm[l][None, :],
            "w_out": w_out[l].astype(BF16),
            "norm_post_mix": norm_post_mix[l][None, :],
            "norm_pre_ffn": norm_pre_ffn[l][None, :],
            "ffn_w_up": ffn_w_up[l].astype(BF16),
            "ffn_conv": ffn_conv[l],
            "ffn_w_down": ffn_w_down[l].astype(BF16),
            "norm_post_ffn": norm_post_ffn[l][None, :],
        }
        yp, st_p = _layer_prompt(yp, lw)
        prompt_new.append(st_p)
        ys, st_s = _layer_sample(ys, lw, state_dn_S[l], state_dn_conv[l], state_ffn_conv[l], pools, page_table, l)
        sample_new.append(st_s)
    p_out = [jnp.stack(a) for a in zip(*prompt_new)]
    s_out = [jnp.stack(a) for a in zip(*sample_new)]
    return (yp, ys, *p_out, *s_out)
```

```python
import functools

import jax
import jax.numpy as jnp
from jax import lax
from jax.experimental import pallas as pl
from jax.experimental.pallas import tpu as pltpu

F32 = jnp.float32
BF16 = jnp.bfloat16

HEAD_DIM = 128
H_DN = 6
H_FOX = 6
H_SB = 4
DN_W = H_DN * HEAD_DIM
FOX_W = H_FOX * HEAD_DIM
SB_W = H_SB * HEAD_DIM
DN_CONV = 4
FFN_CONV = 3
DN_CHUNK = 64
PAGE = 128
EPS = 1e-6
SCALE = HEAD_DIM ** -0.5
LANE = 128

BLK_QKV = 0
BLK_Z = 18
BLK_FQ = 24
BLK_FK = 30
BLK_FV = 36
BLK_GATE = 42
BLK_SQ = 44
BLK_SK = 48
BLK_SV = 52
N_BLK = 56
PACK_W = N_BLK * LANE

VMEM_LIMIT = 56 * 1024 * 1024
PAGES_PER_STEP = 8


def _cparams(*sem):
    return pltpu.CompilerParams(dimension_semantics=sem, vmem_limit_bytes=VMEM_LIMIT)


def _dot(a, b):
    return jnp.dot(a, b, preferred_element_type=F32)


def _dot_nt(a, b):
    return lax.dot_general(a, b, (((1,), (1,)), ((), ())), preferred_element_type=F32)


def _dot_tn(a, b):
    return lax.dot_general(a, b, (((0,), (0,)), ((), ())), preferred_element_type=F32)


def _dot_hi(a, b):
    return jnp.dot(a, b, preferred_element_type=F32, precision=lax.Precision.HIGHEST)


def _split(x):
    hi = x.astype(BF16)
    return hi, (x - hi.astype(F32)).astype(BF16)


def _dot3(a, b):
    return _dot(a[0], b[0]) + (_dot(a[0], b[1]) + _dot(a[1], b[0]))


def _sigmoid(x):
    return 1.0 / (1.0 + jnp.exp(-x))


def _silu(x):
    return x * _sigmoid(x)


def _neg_softplus(z):
    return -(jnp.maximum(z, 0.0) + jnp.log1p(jnp.exp(-jnp.abs(z))))


def _rms(x, w):
    return x * lax.rsqrt(jnp.mean(x * x, axis=-1, keepdims=True) + EPS) * w


def _head_slices(n):
    return [slice(h * HEAD_DIM, (h + 1) * HEAD_DIM) for h in range(n)]


def _norm_mm_kernel(x_ref, g_ref, w_ref, o_ref, hn_ref):
    @pl.when(pl.program_id(1) == 0)
    def _():
        hn_ref[...] = _rms(x_ref[...], g_ref[...]).astype(BF16)

    o_ref[...] = _dot(hn_ref[...], w_ref[...])


def _norm_matmul(x, g, w, *, tm, tn):
    m, d = x.shape
    n = w.shape[1]
    return pl.pallas_call(
        _norm_mm_kernel,
        out_shape=jax.ShapeDtypeStruct((m, n), F32),
        grid=(m // tm, n // tn),
        in_specs=[
            pl.BlockSpec((tm, d), lambda i, j: (i, 0)),
            pl.BlockSpec((1, d), lambda i, j: (0, 0)),
            pl.BlockSpec((d, tn), lambda i, j: (0, j)),
        ],
        out_specs=pl.BlockSpec((tm, tn), lambda i, j: (i, j)),
        scratch_shapes=[pltpu.VMEM((tm, d), BF16)],
        compiler_params=_cparams("parallel", "arbitrary"),
        name="norm_in_proj",
    )(x, g, w)


def _dn_pre_kernel(x_ref, halo_ref, st_ref, w_ref, q_ref, k_ref, v_ref, newst_ref, xc_ref, *, tt):
    t = pl.program_id(1)
    xc_ref[pl.ds(8, tt), :] = x_ref[0]

    @pl.when(t == 0)
    def _():
        xc_ref[5:8, :] = st_ref[0]

    @pl.when(t > 0)
    def _():
        xc_ref[0:8, :] = halo_ref[0]

    for part in range(3):
        for h in range(H_DN):
            cs = slice(part * DN_W + h * HEAD_DIM, part * DN_W + (h + 1) * HEAD_DIM)
            hs = slice(h * HEAD_DIM, (h + 1) * HEAD_DIM)
            y = xc_ref[pl.ds(5, tt), cs] * w_ref[0:1, cs]
            for i in range(1, DN_CONV):
                y = y + xc_ref[pl.ds(5 + i, tt), cs] * w_ref[i:i + 1, cs]
            y = _silu(y)
            if part == 2:
                v_ref[0, :, hs] = y
            else:
                yn = y * lax.rsqrt(jnp.sum(y * y, axis=-1, keepdims=True) + EPS)
                if part == 0:
                    q_ref[0, :, hs] = yn * SCALE
                else:
                    k_ref[0, :, hs] = yn
    newst_ref[0] = xc_ref[pl.ds(tt + 5, DN_CONV - 1), :]


def _dn_pre(proj3, state, w, *, tt):
    b, t, _ = proj3.shape
    c = 3 * DN_W
    hb = tt // 8
    out_shape = [jax.ShapeDtypeStruct((b, t, DN_W), F32)] * 3 + [jax.ShapeDtypeStruct((b, DN_CONV - 1, c), F32)]
    qkv_spec = pl.BlockSpec((1, tt, DN_W), lambda i, j: (i, j, 0))
    return pl.pallas_call(
        functools.partial(_dn_pre_kernel, tt=tt),
        out_shape=out_shape,
        grid=(b, t // tt),
        in_specs=[
            pl.BlockSpec((1, tt, c), lambda i, j: (i, j, 0)),
            pl.BlockSpec((1, 8, c), lambda i, j: (i, jnp.maximum(j * hb - 1, 0), 0)),
            pl.BlockSpec((1, DN_CONV - 1, c), lambda i, j: (i, 0, 0)),
            pl.BlockSpec((DN_CONV, c), lambda i, j: (0, 0)),
        ],
        out_specs=[qkv_spec, qkv_spec, qkv_spec, pl.BlockSpec((1, DN_CONV - 1, c), lambda i, j: (i, 0, 0))],
        scratch_shapes=[pltpu.VMEM((tt + 8, c), F32)],
        compiler_params=_cparams("parallel", "arbitrary"),
        name="dn_conv_prep",
    )(proj3, proj3, state, w)


def _scan_rows(x, row, limit):
    d = 1
    while d < limit:
        x = x + jnp.where(row >= d, pltpu.roll(x, d, axis=0), 0.0)
        d *= 2
    return x


def _gates_kernel(x_ref, bias_ref, alog_ref, gates_ref, cum_ref, gcl_ref, carry_ref, *, tt, chunk):
    t = pl.program_id(1)

    @pl.when(t == 0)
    def _():
        carry_ref[...] = jnp.zeros_like(carry_ref)

    x = x_ref[0] + bias_ref[...]
    lane = lax.broadcasted_iota(jnp.int32, x.shape, 1)
    row = lax.broadcasted_iota(jnp.int32, x.shape, 0)
    lp = jnp.log1p(jnp.exp(-jnp.abs(x)))
    softplus = jnp.maximum(x, 0.0) + lp
    logsig = jnp.minimum(x, 0.0) - lp
    decay = -jnp.exp(alog_ref[...]) * softplus
    out = jnp.where(lane < H_DN, _sigmoid(x),
                    jnp.where(lane < 2 * H_DN, decay,
                              jnp.where(lane < 2 * H_DN + H_FOX, logsig, 0.0)))
    gates_ref[0] = out
    c = _scan_rows(out, row, tt) + carry_ref[...]
    cum_ref[0] = c
    carry_ref[...] = c[tt - 1:tt, :]
    gcl_ref[0] = _scan_rows(out, row % chunk, min(chunk, tt))


def _gates(proj3, bias, alog, *, tt):
    b, t, _ = proj3.shape
    spec = pl.BlockSpec((1, tt, LANE), lambda i, j: (i, j, 0))
    return pl.pallas_call(
        functools.partial(_gates_kernel, tt=tt, chunk=DN_CHUNK),
        out_shape=[jax.ShapeDtypeStruct((b, t, LANE), F32)] * 3,
        grid=(b, t // tt),
        in_specs=[
            pl.BlockSpec((1, tt, LANE), lambda i, j: (i, j, BLK_GATE)),
            pl.BlockSpec((1, LANE), lambda i, j: (0, 0)),
            pl.BlockSpec((1, LANE), lambda i, j: (0, 0)),
        ],
        out_specs=[spec, spec, spec],
        scratch_shapes=[pltpu.VMEM((1, LANE), F32)],
        compiler_params=_cparams("parallel", "arbitrary"),
        name="gate_act",
    )(proj3, bias, alog)


def _gdn_kernel(q_ref, k_ref, v_ref, z_ref, gates_ref, gcl_ref, gclt_ref, s0_ref, nw_ref, o_ref, s_ref, *, c):
    @pl.when(pl.program_id(1) == 0)
    def _():
        s_ref[...] = s0_ref[...]

    ii = lax.broadcasted_iota(jnp.int32, (c, c), 0)
    jj = lax.broadcasted_iota(jnp.int32, (c, c), 1)
    lower = ii >= jj
    strict = ii > jj
    eye = jnp.where(ii == jj, 1.0, 0.0).astype(F32)
    n_sq = c.bit_length() - 2
    heads = range(H_DN)
    hsl = _head_slices(H_DN)
    q = [q_ref[0, :, hs] for hs in hsl]
    k = [k_ref[0, :, hs] for hs in hsl]
    v = [v_ref[0, :, hs] for hs in hsl]
    beta = [gates_ref[0, :, h:h + 1] for h in heads]
    gc = [gcl_ref[0, :, H_DN + h:H_DN + h + 1] for h in heads]
    gr = [gclt_ref[0, 0, h:h + 1, :] for h in heads]
    gl = [g[:, c - 1:c] for g in gr]
    gamma = [jnp.exp(jnp.where(lower, gc[h] - gr[h], -jnp.inf)) for h in heads]
    kb = [k[h] * beta[h] for h in heads]
    kbf = [x.astype(BF16) for x in k]
    kk = [_dot_nt(kb[h].astype(BF16), kbf[h]) for h in heads]
    mm = [jnp.where(strict, -(kk[h] * gamma[h]), 0.0) for h in heads]
    tinv = [eye + m for m in mm]
    ps = [_split(m) for m in mm]
    for _ in range(n_sq):
        ps = [_split(_dot3(x, x)) for x in ps]
        tinv = [tinv[h] + _dot3(_split(tinv[h]), ps[h]) for h in heads]
    egc = [jnp.exp(g) for g in gc]
    rhs = [_split(jnp.concatenate([kb[h] * egc[h], v[h] * beta[h]], axis=1)) for h in heads]
    wu = [_dot3(_split(tinv[h]), rhs[h]) for h in heads]
    qk = [_dot_nt(q[h].astype(BF16), kbf[h]) * gamma[h] for h in heads]
    s = [s_ref[0, h] for h in heads]
    sb = [x.astype(BF16) for x in s]
    v_new = [wu[h][:, HEAD_DIM:] - _dot(wu[h][:, :HEAD_DIM].astype(BF16), sb[h]) for h in heads]
    vnb = [x.astype(BF16) for x in v_new]
    for h in heads:
        k_dec = k[h] * jnp.exp(gl[h] - gc[h])
        s_ref[0, h] = s[h] * jnp.exp(gl[h]) + _dot_tn(k_dec.astype(BF16), vnb[h])
    for h in heads:
        o = _dot((q[h] * egc[h]).astype(BF16), sb[h]) + _dot(qk[h].astype(BF16), vnb[h])
        o_ref[0, :, hsl[h]] = (_rms(o, nw_ref[...]) * _silu(z_ref[0, :, hsl[h]])).astype(o_ref.dtype)


def _gdn(q, k, v, z, z_blk, gates, gcl, gclt, s0, nw):
    b, t, _ = q.shape
    c = DN_CHUNK
    blk = pl.BlockSpec((1, c, DN_W), lambda i, j: (i, j, 0))
    gblk = pl.BlockSpec((1, c, LANE), lambda i, j: (i, j, 0))
    sblk = pl.BlockSpec((1, H_DN, HEAD_DIM, HEAD_DIM), lambda i, j: (i, 0, 0, 0))
    return pl.pallas_call(
        functools.partial(_gdn_kernel, c=c),
        out_shape=[jax.ShapeDtypeStruct((b, t, DN_W), BF16), jax.ShapeDtypeStruct(s0.shape, F32)],
        grid=(b, t // c),
        in_specs=[
            blk, blk, blk,
            pl.BlockSpec((1, c, DN_W), lambda i, j: (i, j, z_blk)),
            gblk, gblk,
            pl.BlockSpec((1, 1, 8, c), lambda i, j: (i, j, 0, 0)),
            sblk,
            pl.BlockSpec((1, HEAD_DIM), lambda i, j: (0, 0)),
        ],
        out_specs=[blk, sblk],
        compiler_params=_cparams("parallel", "arbitrary"),
        name="gated_delta_rule",
    )(q, k, v, z, gates, gcl, gclt, s0, nw)


def _fox_kernel(q_ref, k_ref, v_ref, ct_ref, nw_ref, o_ref, qb_ref, m_ref, l_ref, acc_ref, *, tq):
    qi = pl.program_id(1)
    heads = range(H_FOX)
    hsl = _head_slices(H_FOX)
    row = lax.broadcasted_iota(jnp.int32, (tq, tq), 0)
    col = lax.broadcasted_iota(jnp.int32, (tq, tq), 1)
    ones = jnp.ones((tq, HEAD_DIM), BF16)
    qb_ref[...] = q_ref[0].astype(BF16)
    m_ref[...] = jnp.full_like(m_ref, -jnp.inf)
    l_ref[...] = jnp.zeros_like(l_ref)
    acc_ref[...] = jnp.zeros_like(acc_ref)

    def step(j, diagonal):
        ks = pl.ds(pl.multiple_of(j * tq, tq), tq)
        s = [_dot_nt(qb_ref[:, hsl[h]], k_ref[0, ks, hsl[h]].astype(BF16)) * SCALE - ct_ref[0, j, h:h + 1, :]
             for h in heads]
        if diagonal:
            s = [jnp.where(col <= row, x, -jnp.inf) for x in s]
        m_old = [m_ref[h] for h in heads]
        m_new = [jnp.maximum(m_old[h], jnp.max(s[h], axis=-1, keepdims=True)) for h in heads]
        p = [jnp.exp(s[h] - m_new[h]) for h in heads]
        a = [jnp.exp(m_old[h] - m_new[h]) for h in heads]
        for h in heads:
            pb = p[h].astype(BF16)
            m_ref[h] = m_new[h]
            l_ref[h] = a[h] * l_ref[h] + _dot(pb, ones)
            acc_ref[h] = a[h] * acc_ref[h] + _dot(pb, v_ref[0, ks, hsl[h]].astype(BF16))

    def body(j, carry):
        step(j, False)
        return carry

    lax.fori_loop(0, qi, body, 0)
    step(qi, True)
    for h in heads:
        o_ref[0, :, hsl[h]] = _rms(acc_ref[h] / l_ref[h], nw_ref[...]).astype(o_ref.dtype)


def _fox_prompt(proj3, ct, nw, *, tq):
    b, t, _ = proj3.shape
    return pl.pallas_call(
        functools.partial(_fox_kernel, tq=tq),
        out_shape=jax.ShapeDtypeStruct((b, t, FOX_W), BF16),
        grid=(b, t // tq),
        in_specs=[
            pl.BlockSpec((1, tq, FOX_W), lambda i, j: (i, j, BLK_FQ // H_FOX)),
            pl.BlockSpec((1, t, FOX_W), lambda i, j: (i, 0, BLK_FK // H_FOX)),
            pl.BlockSpec((1, t, FOX_W), lambda i, j: (i, 0, BLK_FV // H_FOX)),
            pl.BlockSpec((1, t // tq, 8, tq), lambda i, j: (i, 0, 0, 0)),
            pl.BlockSpec((1, HEAD_DIM), lambda i, j: (0, 0)),
        ],
        out_specs=pl.BlockSpec((1, tq, FOX_W), lambda i, j: (i, j, 0)),
        scratch_shapes=[
            pltpu.VMEM((tq, FOX_W), BF16),
            pltpu.VMEM((H_FOX, tq, 1), F32),
            pltpu.VMEM((H_FOX, tq, HEAD_DIM), F32),
            pltpu.VMEM((H_FOX, tq, HEAD_DIM), F32),
        ],
        compiler_params=_cparams("parallel", "arbitrary"),
        name="fox_prompt",
    )(proj3, proj3, proj3, ct, nw)


def _suffix_incl(lg, tri):
    hi, lo = _split(lg)
    return _dot(hi, tri) + _dot(lo, tri)


def _sb_kernel(q_ref, k_ref, v_ref, nw_ref, o_ref, qb_ref, r_ref, acc_ref, *, tq):
    qi = pl.program_id(1)
    heads = range(H_SB)
    hsl = _head_slices(H_SB)
    row = lax.broadcasted_iota(jnp.int32, (tq, tq), 0)
    col = lax.broadcasted_iota(jnp.int32, (tq, tq), 1)
    tri = jnp.where(row >= col, 1.0, 0.0).astype(BF16)
    qb_ref[...] = q_ref[0].astype(BF16)
    r_ref[...] = jnp.zeros_like(r_ref)
    acc_ref[...] = jnp.zeros_like(acc_ref)

    def step(jr, diagonal):
        ks = pl.ds(pl.multiple_of((qi - jr) * tq, tq), tq)
        z = [_dot_nt(qb_ref[:, hsl[h]], k_ref[0, ks, hsl[h]].astype(BF16)) * SCALE for h in heads]
        lg = [_neg_softplus(x) for x in z]
        if diagonal:
            lg = [jnp.where(col < row, x, 0.0) for x in lg]
        incl = [_suffix_incl(x, tri) for x in lg]
        a = [jnp.exp(z[h] + incl[h] + r_ref[h]) for h in heads]
        if diagonal:
            a = [jnp.where(col < row, x, 0.0) for x in a]
        for h in heads:
            acc_ref[h] += _dot(a[h].astype(BF16), v_ref[0, ks, hsl[h]].astype(BF16))
            r_ref[h] += incl[h][:, 0:1]

    def body(jr, carry):
        step(jr, False)
        return carry

    step(0, True)
    lax.fori_loop(1, qi + 1, body, 0)
    for h in heads:
        o_ref[0, :, hsl[h]] = _rms(acc_ref[h], nw_ref[...]).astype(o_ref.dtype)


def _sb_prompt(proj3, nw, *, tq):
    b, t, _ = proj3.shape
    return pl.pallas_call(
        functools.partial(_sb_kernel, tq=tq),
        out_shape=jax.ShapeDtypeStruct((b, t, SB_W), BF16),
        grid=(b, t // tq),
        in_specs=[
            pl.BlockSpec((1, tq, SB_W), lambda i, j: (i, j, BLK_SQ // H_SB)),
            pl.BlockSpec((1, t, SB_W), lambda i, j: (i, 0, BLK_SK // H_SB)),
            pl.BlockSpec((1, t, SB_W), lambda i, j: (i, 0, BLK_SV // H_SB)),
            pl.BlockSpec((1, HEAD_DIM), lambda i, j: (0, 0)),
        ],
        out_specs=pl.BlockSpec((1, tq, SB_W), lambda i, j: (i, j, 0)),
        scratch_shapes=[
            pltpu.VMEM((tq, SB_W), BF16),
            pltpu.VMEM((H_SB, tq, 1), F32),
            pltpu.VMEM((H_SB, tq, HEAD_DIM), F32),
        ],
        compiler_params=_cparams("parallel", "arbitrary"),
        name="sb_prompt",
    )(proj3, proj3, proj3, nw)


def _page_rows(page_ref, head_major=False):
    x = page_ref[...] if head_major else jnp.swapaxes(page_ref[...], 0, 1)
    return jnp.concatenate([x[h] for h in range(x.shape[0])], axis=1).astype(BF16)


def _block_diag_q(q_ref, nh, t):
    zero = jnp.zeros((t, HEAD_DIM), F32)
    rows = [jnp.concatenate([q_ref[0, :, h * HEAD_DIM:(h + 1) * HEAD_DIM] if j == h else zero
                             for j in range(nh)], axis=1) for h in range(nh)]
    return jnp.concatenate(rows, axis=0).astype(BF16)


def _rows_per_head(x, nh, t):
    return jnp.concatenate([jnp.broadcast_to(x[h:h + 1, :], (t, x.shape[1])) for h in range(nh)], axis=0)


def _scores(qblk, keys):
    return _dot_nt(qblk, keys) * SCALE


def _weighted_values(w, values, nh, t):
    full = _dot(w.astype(BF16), values)
    return jnp.concatenate([full[h * t:(h + 1) * t, h * HEAD_DIM:(h + 1) * HEAD_DIM] for h in range(nh)], axis=0)


def _fox_dec_kernel(pt_ref, q_ref, kn_ref, vn_ref, cn_ref, nw_ref, *rest, g, t):
    kp, vp, lp = rest[:g], rest[g:2 * g], rest[2 * g:3 * g]
    o_ref = rest[3 * g]
    m_ref, l_ref, acc_ref, off_ref, kpad_ref, vpad_ref = rest[3 * g + 1:]
    nh = H_FOX
    hsl = _head_slices(nh)
    p = pl.program_id(1)

    @pl.when(p == 0)
    def _():
        m_ref[...] = jnp.full_like(m_ref, -jnp.inf)
        l_ref[...] = jnp.zeros_like(l_ref)
        acc_ref[...] = jnp.zeros_like(acc_ref)
        off_ref[...] = jnp.zeros_like(off_ref)

    qblk = _block_diag_q(q_ref, nh, t)
    ii = lax.broadcasted_iota(jnp.int32, (PAGE, PAGE), 0)
    jj = lax.broadcasted_iota(jnp.int32, (PAGE, PAGE), 1)
    upper = jnp.where(ii <= jj, 1.0, 0.0).astype(F32)

    def update(s, weighted_values):
        m = m_ref[...]
        m_new = jnp.maximum(m, jnp.max(s, axis=-1, keepdims=True))
        pr = jnp.exp(s - m_new)
        a = jnp.exp(m - m_new)
        l_ref[...] = a * l_ref[...] + jnp.sum(pr, axis=-1, keepdims=True)
        acc_ref[...] = a * acc_ref[...] + weighted_values(pr)
        m_ref[...] = m_new

    cw = [_dot_hi(lp[i][...], upper) for i in range(g)]
    off = off_ref[...]
    s_pages = []
    for i in range(g):
        s_pages.append(_scores(qblk, _page_rows(kp[i], True)) - _rows_per_head(off + cw[i], nh, t))
        off = off + cw[i][:, PAGE - 1:PAGE]
    off_ref[...] = off
    values = jnp.concatenate([_page_rows(vp[i], True) for i in range(g)], axis=0)
    update(jnp.concatenate(s_pages, axis=1), lambda pr: _weighted_values(pr, values, nh, t))

    @pl.when(p == pl.num_programs(1) - 1)
    def _():
        kpad_ref[...] = jnp.zeros_like(kpad_ref)
        vpad_ref[...] = jnp.zeros_like(vpad_ref)
        kpad_ref[0:t, :] = kn_ref[0]
        vpad_ref[0:t, :] = vn_ref[0]
        c = off_ref[...] + cn_ref[0]
        s = _scores(qblk, kpad_ref[...].astype(BF16)) - _rows_per_head(c, nh, t)
        r = lax.broadcasted_iota(jnp.int32, s.shape, 0)
        cc = lax.broadcasted_iota(jnp.int32, s.shape, 1)
        s = jnp.where(cc <= r % t, s, -jnp.inf)
        update(s, lambda pr: _weighted_values(pr, vpad_ref[...].astype(BF16), nh, t))
        o = acc_ref[...] / l_ref[...]
        for h in range(nh):
            o_ref[0, :, hsl[h]] = _rms(o[h * t:(h + 1) * t], nw_ref[...]).astype(o_ref.dtype)


def _fox_dec(page_table, proj3, cnt, nw, k_pool, v_pool, lf_pool, layer, *, g):
    b, t, _ = proj3.shape
    npg = page_table.shape[1]
    nh = H_FOX

    def page_spec(shape, i):
        return pl.BlockSpec((None, None) + shape, lambda bi, p, pt, i=i: (layer, pt[bi, p * g + i], 0, 0, 0)[:2 + len(shape)])

    in_specs = [
        pl.BlockSpec((1, t, FOX_W), lambda bi, p, pt: (bi, 0, BLK_FQ // nh)),
        pl.BlockSpec((1, t, FOX_W), lambda bi, p, pt: (bi, 0, BLK_FK // nh)),
        pl.BlockSpec((1, t, FOX_W), lambda bi, p, pt: (bi, 0, BLK_FV // nh)),
        pl.BlockSpec((1, 8, PAGE), lambda bi, p, pt: (bi, 0, 0)),
        pl.BlockSpec((1, HEAD_DIM), lambda bi, p, pt: (0, 0)),
    ]
    in_specs += [page_spec((nh, PAGE, HEAD_DIM), i) for i in range(g)]
    in_specs += [page_spec((nh, PAGE, HEAD_DIM), i) for i in range(g)]
    in_specs += [page_spec((8, PAGE), i) for i in range(g)]
    grid_spec = pltpu.PrefetchScalarGridSpec(
        num_scalar_prefetch=1,
        grid=(b, npg // g),
        in_specs=in_specs,
        out_specs=pl.BlockSpec((1, t, FOX_W), lambda bi, p, pt: (bi, 0, 0)),
        scratch_shapes=[
            pltpu.VMEM((nh * t, 1), F32),
            pltpu.VMEM((nh * t, 1), F32),
            pltpu.VMEM((nh * t, HEAD_DIM), F32),
            pltpu.VMEM((8, 1), F32),
            pltpu.VMEM((PAGE, FOX_W), F32),
            pltpu.VMEM((PAGE, FOX_W), F32),
        ],
    )
    return pl.pallas_call(
        functools.partial(_fox_dec_kernel, g=g, t=t),
        out_shape=jax.ShapeDtypeStruct((b, t, FOX_W), BF16),
        grid_spec=grid_spec,
        compiler_params=_cparams("parallel", "arbitrary"),
        name="fox_paged",
    )(page_table, proj3, proj3, proj3, cnt, nw, *([k_pool] * g), *([v_pool] * g), *([lf_pool] * g))


def _sb_dec_kernel(pt_ref, q_ref, kn_ref, vn_ref, nw_ref, *rest, g, t):
    kp, vp = rest[:g], rest[g:2 * g]
    o_ref = rest[2 * g]
    rsum_ref, acc_ref, kpad_ref, vpad_ref = rest[2 * g + 1:]
    nh = H_SB
    hsl = _head_slices(nh)
    p = pl.program_id(1)
    ii = lax.broadcasted_iota(jnp.int32, (PAGE, PAGE), 0)
    jj = lax.broadcasted_iota(jnp.int32, (PAGE, PAGE), 1)
    tri = jnp.where(ii >= jj, 1.0, 0.0).astype(BF16)
    qblk = _block_diag_q(q_ref, nh, t)

    @pl.when(p == 0)
    def _():
        kpad_ref[...] = jnp.zeros_like(kpad_ref)
        vpad_ref[...] = jnp.zeros_like(vpad_ref)
        kpad_ref[0:t, :] = kn_ref[0]
        vpad_ref[0:t, :] = vn_ref[0]
        z = _scores(qblk, kpad_ref[...].astype(BF16))
        r = lax.broadcasted_iota(jnp.int32, z.shape, 0)
        cc = lax.broadcasted_iota(jnp.int32, z.shape, 1)
        valid = cc < r % t
        incl = _suffix_incl(jnp.where(valid, _neg_softplus(z), 0.0), tri)
        a = jnp.where(valid, jnp.exp(z + incl), 0.0)
        rsum_ref[...] = incl[:, 0:1]
        acc_ref[...] = _weighted_values(a, vpad_ref[...].astype(BF16), nh, t)

    zs, incls = [], []
    for i in range(g):
        z = _scores(qblk, _page_rows(kp[i]))
        zs.append(z)
        incls.append(_suffix_incl(_neg_softplus(z), tri))
    rsum = rsum_ref[...]
    weights = []
    for i in range(g):
        weights.append(jnp.exp(zs[i] + incls[i] + rsum))
        rsum = rsum + incls[i][:, 0:1]
    rsum_ref[...] = rsum
    values = jnp.concatenate([_page_rows(vp[i]) for i in range(g)], axis=0)
    acc_ref[...] += _weighted_values(jnp.concatenate(weights, axis=1), values, nh, t)

    @pl.when(p == pl.num_programs(1) - 1)
    def _():
        acc = acc_ref[...]
        for h in range(nh):
            o_ref[0, :, hsl[h]] = _rms(acc[h * t:(h + 1) * t], nw_ref[...]).astype(o_ref.dtype)


def _sb_dec(page_table, proj3, nw, k_pool, v_pool, layer, *, g):
    b, t, _ = proj3.shape
    npg = page_table.shape[1]
    nh = H_SB

    def page_spec(i):
        return pl.BlockSpec((None, None, PAGE, nh, HEAD_DIM),
                            lambda bi, p, pt, i=i: (layer, pt[bi, npg - 1 - (p * g + i)], 0, 0, 0))

    in_specs = [
        pl.BlockSpec((1, t, SB_W), lambda bi, p, pt: (bi, 0, BLK_SQ // nh)),
        pl.BlockSpec((1, t, SB_W), lambda bi, p, pt: (bi, 0, BLK_SK // nh)),
        pl.BlockSpec((1, t, SB_W), lambda bi, p, pt: (bi, 0, BLK_SV // nh)),
        pl.BlockSpec((1, HEAD_DIM), lambda bi, p, pt: (0, 0)),
    ]
    in_specs += [page_spec(i) for i in range(g)] * 2
    grid_spec = pltpu.PrefetchScalarGridSpec(
        num_scalar_prefetch=1,
        grid=(b, npg // g),
        in_specs=in_specs,
        out_specs=pl.BlockSpec((1, t, SB_W), lambda bi, p, pt: (bi, 0, 0)),
        scratch_shapes=[
            pltpu.VMEM((nh * t, 1), F32),
            pltpu.VMEM((nh * t, HEAD_DIM), F32),
            pltpu.VMEM((PAGE, SB_W), F32),
            pltpu.VMEM((PAGE, SB_W), F32),
        ],
    )
    return pl.pallas_call(
        functools.partial(_sb_dec_kernel, g=g, t=t),
        out_shape=jax.ShapeDtypeStruct((b, t, SB_W), BF16),
        grid_spec=grid_spec,
        compiler_params=_cparams("parallel", "arbitrary"),
        name="sb_paged",
    )(page_table, proj3, proj3, proj3, nw, *([k_pool] * g), *([v_pool] * g))


def _head_major_kernel(*refs, nh):
    x_ref, o_ref = refs[0], refs[-1]
    for h in range(nh):
        o_ref[0, h] = x_ref[0, :, h * HEAD_DIM:(h + 1) * HEAD_DIM]


def _head_major_rows(proj3, col_blk, nh, layer, depth, prev, *, tt):
    b, t, _ = proj3.shape
    w = nh * HEAD_DIM
    in_specs = [pl.BlockSpec((1, tt, w), lambda i, j: (i, j, col_blk * LANE // w))]
    args = [proj3]
    aliases = {}
    if prev is not None:
        in_specs.append(pl.BlockSpec(memory_space=pl.ANY))
        args.append(prev)
        aliases = {1: 0}
    return pl.pallas_call(
        functools.partial(_head_major_kernel, nh=nh),
        out_shape=jax.ShapeDtypeStruct((depth, b, nh, t, HEAD_DIM), F32),
        grid=(b, t // tt),
        in_specs=in_specs,
        out_specs=pl.BlockSpec((None, 1, nh, tt, HEAD_DIM), lambda i, j: (layer, i, 0, j, 0)),
        input_output_aliases=aliases,
        compiler_params=_cparams("parallel", "parallel"),
        name="head_major_rows",
    )(*args)


def _out_proj_kernel(x_ref, a_ref, b_ref, c_ref, w_ref, g_ref, o_ref):
    acc = _dot(a_ref[...], w_ref[0:DN_W, :])
    acc += _dot(b_ref[...], w_ref[DN_W:DN_W + FOX_W, :])
    acc += _dot(c_ref[...], w_ref[DN_W + FOX_W:, :])
    o_ref[...] = x_ref[...] + _rms(acc, g_ref[...])


def _out_proj(x, oa, ob, oc, w, g, *, tm):
    m, d = x.shape
    return pl.pallas_call(
        _out_proj_kernel,
        out_shape=jax.ShapeDtypeStruct((m, d), F32),
        grid=(m // tm,),
        in_specs=[
            pl.BlockSpec((tm, d), lambda i: (i, 0)),
            pl.BlockSpec((tm, DN_W), lambda i: (i, 0)),
            pl.BlockSpec((tm, FOX_W), lambda i: (i, 0)),
            pl.BlockSpec((tm, SB_W), lambda i: (i, 0)),
            pl.BlockSpec(w.shape, lambda i: (0, 0)),
            pl.BlockSpec((1, d), lambda i: (0, 0)),
        ],
        out_specs=pl.BlockSpec((tm, d), lambda i: (i, 0)),
        compiler_params=_cparams("parallel"),
        name="out_proj",
    )(x, oa, ob, oc, w, g)


def _ffn_tail(c, x_ref, gpost_ref, acc_ref, o_ref):
    @pl.when(c == pl.num_programs(1) - 1)
    def _():
        o_ref[...] = x_ref[...] + _rms(acc_ref[...], gpost_ref[...])


def _ffn_prompt_kernel(x_ref, halo_ref, gpre_ref, wg_ref, wu_ref, cg_ref, cu_ref, wd_ref, gpost_ref,
                       o_ref, st_ref, hn_ref, acc_ref, ug_ref, uu_ref, *, tm, tiles_per_seq):
    i = pl.program_id(0)
    c = pl.program_id(1)

    @pl.when(c == 0)
    def _():
        hn_ref[pl.ds(8, tm), :] = _rms(x_ref[...], gpre_ref[...]).astype(BF16)
        keep = jnp.where(i % tiles_per_seq == 0, 0.0, 1.0)
        hn_ref[0:8, :] = (_rms(halo_ref[...], gpre_ref[...]) * keep).astype(BF16)
        acc_ref[...] = jnp.zeros_like(acc_ref)

    hn = hn_ref[...]
    ug_ref[...] = _dot(hn, wg_ref[...])
    uu_ref[...] = _dot(hn, wu_ref[...])

    def conv(u_ref, w_ref):
        y = u_ref[pl.ds(8 - (FFN_CONV - 1), tm), :] * w_ref[0:1, :]
        for k in range(1, FFN_CONV):
            y = y + u_ref[pl.ds(8 - (FFN_CONV - 1) + k, tm), :] * w_ref[k:k + 1, :]
        return y

    act = _silu(conv(ug_ref, cg_ref)) * conv(uu_ref, cu_ref)
    acc_ref[...] += _dot(act.astype(BF16), wd_ref[...])
    st_ref[0, 0] = ug_ref[pl.ds(tm + 8 - (FFN_CONV - 1), FFN_CONV - 1), :]
    st_ref[0, 1] = uu_ref[pl.ds(tm + 8 - (FFN_CONV - 1), FFN_CONV - 1), :]
    _ffn_tail(c, x_ref, gpost_ref, acc_ref, o_ref)


def _ffn_prompt(x, seq_len, gpre, w_up, conv_w, w_down, gpost, *, tm, tc):
    m, d = x.shape
    dff = w_down.shape[0]
    nc = dff // tc
    tps = seq_len // tm
    hb = tm // 8
    return pl.pallas_call(
        functools.partial(_ffn_prompt_kernel, tm=tm, tiles_per_seq=tps),
        out_shape=[jax.ShapeDtypeStruct((m, d), F32),
                   jax.ShapeDtypeStruct((m // tm, 2, FFN_CONV - 1, dff), F32)],
        grid=(m // tm, nc),
        in_specs=[
            pl.BlockSpec((tm, d), lambda i, c: (i, 0)),
            pl.BlockSpec((8, d), lambda i, c: (jnp.maximum(i * hb - 1, 0), 0)),
            pl.BlockSpec((1, d), lambda i, c: (0, 0)),
            pl.BlockSpec((d, tc), lambda i, c: (0, c)),
            pl.BlockSpec((d, tc), lambda i, c: (0, nc + c)),
            pl.BlockSpec((FFN_CONV, tc), lambda i, c: (0, c)),
            pl.BlockSpec((FFN_CONV, tc), lambda i, c: (0, nc + c)),
            pl.BlockSpec((tc, d), lambda i, c: (c, 0)),
            pl.BlockSpec((1, d), lambda i, c: (0, 0)),
        ],
        out_specs=[
            pl.BlockSpec((tm, d), lambda i, c: (i, 0)),
            pl.BlockSpec((1, 2, FFN_CONV - 1, tc), lambda i, c: (i, 0, 0, c)),
        ],
        scratch_shapes=[
            pltpu.VMEM((tm + 8, d), BF16),
            pltpu.VMEM((tm, d), F32),
            pltpu.VMEM((tm + 8, tc), F32),
            pltpu.VMEM((tm + 8, tc), F32),
        ],
        compiler_params=_cparams("parallel", "arbitrary"),
        name="conv_ffn_prompt",
    )(x, x, gpre, w_up, w_up, conv_w, conv_w, w_down, gpost)


def _ffn_dec_kernel(x_ref, gpre_ref, wg_ref, wu_ref, cg_ref, cu_ref, wd_ref, gpost_ref,
                    s0g_ref, s1g_ref, s0u_ref, s1u_ref,
                    o_ref, ugo_ref, uuo_ref, hn_ref, acc_ref, ug_ref, uu_ref, *, tm, t):
    c = pl.program_id(1)

    @pl.when(c == 0)
    def _():
        hn_ref[...] = _rms(x_ref[...], gpre_ref[...]).astype(BF16)
        acc_ref[...] = jnp.zeros_like(acc_ref)
        ug_ref[0:8, :] = jnp.zeros((8, ug_ref.shape[1]), F32)
        uu_ref[0:8, :] = jnp.zeros((8, uu_ref.shape[1]), F32)

    hn = hn_ref[...]
    ug = _dot(hn, wg_ref[...])
    uu = _dot(hn, wu_ref[...])
    ugo_ref[...] = ug
    uuo_ref[...] = uu
    ug_ref[pl.ds(8, tm), :] = ug
    uu_ref[pl.ds(8, tm), :] = uu
    pos = lax.broadcasted_iota(jnp.int32, ug.shape, 0) % t

    def conv(u, u_ref, s0_ref, s1_ref, w_ref):
        um1 = jnp.where(pos >= 1, u_ref[pl.ds(7, tm), :], s1_ref[...])
        um2 = jnp.where(pos >= 2, u_ref[pl.ds(6, tm), :], jnp.where(pos == 0, s0_ref[...], s1_ref[...]))
        return um2 * w_ref[0:1, :] + um1 * w_ref[1:2, :] + u * w_ref[2:3, :]

    act = _silu(conv(ug, ug_ref, s0g_ref, s1g_ref, cg_ref)) * conv(uu, uu_ref, s0u_ref, s1u_ref, cu_ref)
    acc_ref[...] += _dot(act.astype(BF16), wd_ref[...])
    _ffn_tail(c, x_ref, gpost_ref, acc_ref, o_ref)


def _ffn_dec(x, t, gpre, w_up, conv_w, w_down, gpost, s0e, s1e, *, tc):
    m, d = x.shape
    dff = w_down.shape[0]
    nc = dff // tc
    tm = m
    st = lambda off: pl.BlockSpec((tm, tc), lambda i, c, off=off: (0, off * nc + c))
    return pl.pallas_call(
        functools.partial(_ffn_dec_kernel, tm=tm, t=t),
        out_shape=[jax.ShapeDtypeStruct((m, d), F32),
                   jax.ShapeDtypeStruct((m, dff), F32), jax.ShapeDtypeStruct((m, dff), F32)],
        grid=(1, nc),
        in_specs=[
            pl.BlockSpec((tm, d), lambda i, c: (0, 0)),
            pl.BlockSpec((1, d), lambda i, c: (0, 0)),
            pl.BlockSpec((d, tc), lambda i, c: (0, c)),
            pl.BlockSpec((d, tc), lambda i, c: (0, nc + c)),
            pl.BlockSpec((FFN_CONV, tc), lambda i, c: (0, c)),
            pl.BlockSpec((FFN_CONV, tc), lambda i, c: (0, nc + c)),
            pl.BlockSpec((tc, d), lambda i, c: (c, 0)),
            pl.BlockSpec((1, d), lambda i, c: (0, 0)),
            st(0), st(0), st(1), st(1),
        ],
        out_specs=[
            pl.BlockSpec((tm, d), lambda i, c: (0, 0)),
            pl.BlockSpec((tm, tc), lambda i, c: (0, c)),
            pl.BlockSpec((tm, tc), lambda i, c: (0, c)),
        ],
        scratch_shapes=[
            pltpu.VMEM((tm, d), BF16),
            pltpu.VMEM((tm, d), F32),
            pltpu.VMEM((tm + 8, tc), F32),
            pltpu.VMEM((tm + 8, tc), F32),
        ],
        compiler_params=_cparams("arbitrary", "arbitrary"),
        name="conv_ffn_sample",
    )(x, gpre, w_up, w_up, conv_w, conv_w, w_down, gpost, s0e, s1e, s0e, s1e)


def _pack_w_in(w_in):
    d = w_in.shape[0]
    sizes = (3 * DN_W, H_DN, H_DN, DN_W, FOX_W, FOX_W, FOX_W, H_FOX, SB_W, SB_W, SB_W)
    parts, start = [], 0
    for s in sizes:
        parts.append(w_in[:, start:start + s])
        start += s
    qkv, dn_b, dn_a, dn_z, fq, fk, fv, ff, sq, sk, sv = parts
    gate = jnp.concatenate([dn_b, dn_a, ff, jnp.zeros((d, 2 * LANE - 2 * H_DN - H_FOX), w_in.dtype)], axis=1)
    return jnp.concatenate([qkv, dn_z, fq, fk, fv, gate, sq, sk, sv], axis=1).astype(BF16)


def _lane_row(pairs):
    row = jnp.zeros((LANE,), F32)
    for off, v in pairs:
        row = row.at[off:off + v.shape[0]].set(v.astype(F32))
    return row[None, :]


def _mixers_common(x3, lw, dn_state, tm, tt):
    b, t, d = x3.shape
    proj = _norm_matmul(x3.reshape(b * t, d), lw["norm_pre_mix"], lw["w_in"], tm=tm, tn=1024)
    proj3 = proj.reshape(b, t, PACK_W)
    q, k, v, dn_buf_new = _dn_pre(proj3, dn_state, lw["dn_conv"], tt=tt)
    gates, cum, gcl = _gates(proj3, lw["gate_bias"], lw["alog_row"], tt=tt)
    return proj3, q, k, v, dn_buf_new, gates, cum, gcl


def _gclt(gcl, c):
    b, t, _ = gcl.shape
    g = gcl[:, :, H_DN:2 * H_DN].reshape(b, t // c, c, H_DN)
    g = jnp.swapaxes(g, 2, 3)
    return jnp.pad(g, ((0, 0), (0, 0), (0, 8 - H_DN), (0, 0)))


def _new_rows(proj3, gates):
    b, t, _ = proj3.shape

    def cols(blk, w, nh):
        return proj3[:, :, blk * LANE:blk * LANE + w].reshape(b, t, nh, HEAD_DIM)

    return (cols(BLK_FK, FOX_W, H_FOX), cols(BLK_FV, FOX_W, H_FOX), gates[:, :, 2 * H_DN:2 * H_DN + H_FOX],
            cols(BLK_SK, SB_W, H_SB), cols(BLK_SV, SB_W, H_SB))


def _layer_prompt(x3, lw, layer, depth, fox_bufs):
    b, t, d = x3.shape
    tq = 256
    tm_ffn = 512
    zeros_dn = jnp.zeros((b, DN_CONV - 1, 3 * DN_W), F32)
    proj3, q, k, v, dn_buf_new, gates, cum, gcl = _mixers_common(x3, lw, zeros_dn, tm=1024, tt=256)
    s0 = jnp.zeros((b, H_DN, HEAD_DIM, HEAD_DIM), F32)
    o_a, s_new = _gdn(q, k, v, proj3, BLK_Z // H_DN, gates, gcl, _gclt(gcl, DN_CHUNK), s0, lw["dn_norm"])
    ct = cum[:, :, 2 * H_DN:2 * H_DN + H_FOX].reshape(b, t // tq, tq, H_FOX)
    ct = jnp.pad(jnp.swapaxes(ct, 2, 3), ((0, 0), (0, 0), (0, 8 - H_FOX), (0, 0)))
    o_b = _fox_prompt(proj3, ct, lw["fox_norm"], tq=tq)
    o_c = _sb_prompt(proj3, lw["sb_norm"], tq=tq)
    m = b * t
    x1 = _out_proj(x3.reshape(m, d), o_a.reshape(m, DN_W), o_b.reshape(m, FOX_W), o_c.reshape(m, SB_W),
                   lw["w_out"], lw["norm_post_mix"], tm=512)
    y, st = _ffn_prompt(x1, t, lw["norm_pre_ffn"], lw["ffn_w_up"], lw["ffn_conv"], lw["ffn_w_down"],
                        lw["norm_post_ffn"], tm=tm_ffn, tc=512)
    tps = t // tm_ffn
    ffn_buf_new = jnp.swapaxes(st[tps - 1::tps], 1, 2).reshape(b, FFN_CONV - 1, -1)
    _, _, logf, sk, sv = _new_rows(proj3, gates)
    fox_bufs = (_head_major_rows(proj3, BLK_FK, H_FOX, layer, depth, fox_bufs[0], tt=512),
                _head_major_rows(proj3, BLK_FV, H_FOX, layer, depth, fox_bufs[1], tt=512))
    return y.reshape(b, t, d), (s_new, dn_buf_new, logf, sk, sv, ffn_buf_new), fox_bufs


def _layer_sample(x3, lw, dn_s0, dn_buf, ffn_buf, pools, page_table, layer):
    b, t, d = x3.shape
    c = DN_CHUNK
    fox_k, fox_v, fox_lf, sb_k, sb_v = pools
    proj3, q, k, v, dn_buf_new, gates, cum, gcl = _mixers_common(x3, lw, dn_buf, tm=b * t, tt=t)
    padt = ((0, 0), (0, c - t), (0, 0))
    z = proj3[:, :, BLK_Z * LANE:BLK_Z * LANE + DN_W]
    gcl_pad = jnp.pad(gcl, padt, mode="edge")
    o_a, s_new = _gdn(jnp.pad(q, padt), jnp.pad(k, padt), jnp.pad(v, padt), jnp.pad(z, padt), 0,
                      jnp.pad(gates, padt), gcl_pad, _gclt(gcl_pad, c), dn_s0, lw["dn_norm"])
    o_a = o_a[:, :t]
    cnt = jnp.swapaxes(cum[:, :, 2 * H_DN:2 * H_DN + H_FOX], 1, 2)
    cnt = jnp.pad(cnt, ((0, 0), (0, 8 - H_FOX), (0, PAGE - t)))
    o_b = _fox_dec(page_table, proj3, cnt, lw["fox_norm"], fox_k, fox_v, fox_lf, layer, g=PAGES_PER_STEP)
    o_c = _sb_dec(page_table, proj3, lw["sb_norm"], sb_k, sb_v, layer, g=PAGES_PER_STEP)
    m = b * t
    x1 = _out_proj(x3.reshape(m, d), o_a.reshape(m, DN_W), o_b.reshape(m, FOX_W), o_c.reshape(m, SB_W),
                   lw["w_out"], lw["norm_post_mix"], tm=m)
    s0e = jnp.repeat(ffn_buf[:, 0, :], t, axis=0)
    s1e = jnp.repeat(ffn_buf[:, 1, :], t, axis=0)
    y, ug, uu = _ffn_dec(x1, t, lw["norm_pre_ffn"], lw["ffn_w_up"], lw["ffn_conv"], lw["ffn_w_down"],
                         lw["norm_post_ffn"], s0e, s1e, tc=512)
    u = jnp.concatenate([ug, uu], axis=1).reshape(b, t, -1)
    ffn_buf_new = u[:, t - (FFN_CONV - 1):]
    fk, fv, logf, sk, sv = _new_rows(proj3, gates)
    return y.reshape(b, t, d), (s_new, dn_buf_new, fk, fv, logf, sk, sv, ffn_buf_new)


def kernel(x_prompt, x_sample, state_dn_S, state_dn_conv, cache_fox_k, cache_fox_v, cache_fox_logf, cache_sb_k, cache_sb_v, state_ffn_conv, page_table, norm_pre_mix, w_in, dn_conv, dn_A_log, dn_dt_bias, dn_norm, fox_b_f, fox_norm, sb_norm, w_out, norm_post_mix, norm_pre_ffn, ffn_w_up, ffn_conv, ffn_w_down, norm_post_ffn):
    depth = w_in.shape[0]
    yp, ys = x_prompt, x_sample
    prompt_new, sample_new = [], []
    fox_bufs = (None, None)
    lf_t = jnp.pad(jnp.swapaxes(cache_fox_logf, 2, 3), ((0, 0), (0, 0), (0, 8 - H_FOX), (0, 0)))
    pools = (jnp.swapaxes(cache_fox_k, 2, 3), jnp.swapaxes(cache_fox_v, 2, 3), lf_t, cache_sb_k, cache_sb_v)
    for l in range(depth):
        lw = {
            "norm_pre_mix": norm_pre_mix[l][None, :],
            "w_in": _pack_w_in(w_in[l]),
            "dn_conv": dn_conv[l],
            "gate_bias": _lane_row([(H_DN, dn_dt_bias[l]), (2 * H_DN, fox_b_f[l])]),
            "alog_row": _lane_row([(H_DN, dn_A_log[l])]),
            "dn_norm": dn_norm[l][None, :],
            "fox_norm": fox_norm[l][None, :],
            "sb_norm": sb_norm[l][None, :],
            "w_out": w_out[l].astype(BF16),
            "norm_post_mix": norm_post_mix[l][None, :],
            "norm_pre_ffn": norm_pre_ffn[l][None, :],
            "ffn_w_up": ffn_w_up[l].astype(BF16),
            "ffn_conv": ffn_conv[l],
            "ffn_w_down": ffn_w_down[l].astype(BF16),
            "norm_post_ffn": norm_post_ffn[l][None, :],
        }
        yp, st_p, fox_bufs = _layer_prompt(yp, lw, l, depth, fox_bufs)
        prompt_new.append(st_p)
        ys, st_s = _layer_sample(ys, lw, state_dn_S[l], state_dn_conv[l], state_ffn_conv[l], pools, page_table, l)
        sample_new.append(st_s)
    dn_s_p, dn_conv_p, logf_p, sk_p, sv_p, ffn_p = [jnp.stack(a) for a in zip(*prompt_new)]
    fk_p, fv_p = [jnp.swapaxes(x, 2, 3) for x in fox_bufs]
    s_out = [jnp.stack(a) for a in zip(*sample_new)]
    return (yp, ys, dn_s_p, dn_conv_p, fk_p, fv_p, logf_p, sk_p, sv_p, ffn_p, *s_out)
```

```python
import functools

import jax
import jax.numpy as jnp
from jax import lax
from jax.experimental import pallas as pl
from jax.experimental.pallas import tpu as pltpu

F32 = jnp.float32
BF16 = jnp.bfloat16

HEAD_DIM = 128
H_DN = 6
H_FOX = 6
H_SB = 4
DN_W = H_DN * HEAD_DIM
FOX_W = H_FOX * HEAD_DIM
SB_W = H_SB * HEAD_DIM
DN_CONV = 4
FFN_CONV = 3
DN_CHUNK = 64
PAGE = 128
EPS = 1e-6
SCALE = HEAD_DIM ** -0.5
LANE = 128

BLK_QKV = 0
BLK_Z = 18
BLK_FQ = 24
BLK_FK = 30
BLK_FV = 36
BLK_GATE = 42
BLK_SQ = 44
BLK_SK = 48
BLK_SV = 52
N_BLK = 56
PACK_W = N_BLK * LANE

VMEM_LIMIT = 56 * 1024 * 1024
GDN_CHUNKS_PER_STEP = 4
PAGES_PER_STEP = 8


def _cparams(*sem):
    return pltpu.CompilerParams(dimension_semantics=sem, vmem_limit_bytes=VMEM_LIMIT)


def _dot(a, b):
    return jnp.dot(a, b, preferred_element_type=F32)


def _dot_nt(a, b):
    return lax.dot_general(a, b, (((1,), (1,)), ((), ())), preferred_element_type=F32)


def _dot_tn(a, b):
    return lax.dot_general(a, b, (((0,), (0,)), ((), ())), preferred_element_type=F32)


def _dot_hi(a, b):
    return jnp.dot(a, b, preferred_element_type=F32, precision=lax.Precision.HIGHEST)


def _split(x):
    hi = x.astype(BF16)
    return hi, (x - hi.astype(F32)).astype(BF16)


def _dot3(a, b):
    return _dot(a[0], b[0]) + (_dot(a[0], b[1]) + _dot(a[1], b[0]))


def _sigmoid(x):
    return 1.0 / (1.0 + jnp.exp(-x))


def _silu(x):
    return x * _sigmoid(x)


def _neg_softplus(z):
    return -(jnp.maximum(z, 0.0) + jnp.log(1.0 + jnp.exp(-jnp.abs(z))))


def _rms(x, w):
    return x * lax.rsqrt(jnp.mean(x * x, axis=-1, keepdims=True) + EPS) * w


def _head_slices(n):
    return [slice(h * HEAD_DIM, (h + 1) * HEAD_DIM) for h in range(n)]


def _norm_mm_kernel(x_ref, g_ref, w_ref, o_ref, hn_ref):
    @pl.when(pl.program_id(1) == 0)
    def _():
        hn_ref[...] = _rms(x_ref[...], g_ref[...]).astype(BF16)

    o_ref[...] = _dot_nt(hn_ref[...], w_ref[...])


def _norm_matmul(x, g, wt, *, tm, tn):
    m, d = x.shape
    n = wt.shape[0]
    return pl.pallas_call(
        _norm_mm_kernel,
        out_shape=jax.ShapeDtypeStruct((m, n), F32),
        grid=(m // tm, n // tn),
        in_specs=[
            pl.BlockSpec((tm, d), lambda i, j: (i, 0)),
            pl.BlockSpec((1, d), lambda i, j: (0, 0)),
            pl.BlockSpec((tn, d), lambda i, j: (j, 0)),
        ],
        out_specs=pl.BlockSpec((tm, tn), lambda i, j: (i, j)),
        scratch_shapes=[pltpu.VMEM((tm, d), BF16)],
        compiler_params=_cparams("parallel", "arbitrary"),
        name="norm_in_proj",
    )(x, g, wt)


def _dn_pre_kernel(x_ref, halo_ref, st_ref, w_ref, q_ref, k_ref, v_ref, newst_ref, xc_ref, *, tt):
    t = pl.program_id(1)
    xc_ref[pl.ds(8, tt), :] = x_ref[0]

    @pl.when(t == 0)
    def _():
        xc_ref[5:8, :] = st_ref[0]

    @pl.when(t > 0)
    def _():
        xc_ref[0:8, :] = halo_ref[0]

    for part in range(3):
        for h in range(H_DN):
            cs = slice(part * DN_W + h * HEAD_DIM, part * DN_W + (h + 1) * HEAD_DIM)
            hs = slice(h * HEAD_DIM, (h + 1) * HEAD_DIM)
            y = xc_ref[pl.ds(5, tt), cs] * w_ref[0:1, cs]
            for i in range(1, DN_CONV):
                y = y + xc_ref[pl.ds(5 + i, tt), cs] * w_ref[i:i + 1, cs]
            y = _silu(y)
            if part == 2:
                v_ref[0, :, hs] = y
            else:
                yn = y * lax.rsqrt(jnp.sum(y * y, axis=-1, keepdims=True) + EPS)
                if part == 0:
                    q_ref[0, :, hs] = yn * SCALE
                else:
                    k_ref[0, :, hs] = yn
    newst_ref[0] = xc_ref[pl.ds(tt + 5, DN_CONV - 1), :]


def _dn_pre(proj3, state, w, *, tt):
    b, t, _ = proj3.shape
    c = 3 * DN_W
    hb = tt // 8
    out_shape = [jax.ShapeDtypeStruct((b, t, DN_W), F32)] * 3 + [jax.ShapeDtypeStruct((b, DN_CONV - 1, c), F32)]
    qkv_spec = pl.BlockSpec((1, tt, DN_W), lambda i, j: (i, j, 0))
    return pl.pallas_call(
        functools.partial(_dn_pre_kernel, tt=tt),
        out_shape=out_shape,
        grid=(b, t // tt),
        in_specs=[
            pl.BlockSpec((1, tt, c), lambda i, j: (i, j, 0)),
            pl.BlockSpec((1, 8, c), lambda i, j: (i, jnp.maximum(j * hb - 1, 0), 0)),
            pl.BlockSpec((1, DN_CONV - 1, c), lambda i, j: (i, 0, 0)),
            pl.BlockSpec((DN_CONV, c), lambda i, j: (0, 0)),
        ],
        out_specs=[qkv_spec, qkv_spec, qkv_spec, pl.BlockSpec((1, DN_CONV - 1, c), lambda i, j: (i, 0, 0))],
        scratch_shapes=[pltpu.VMEM((tt + 8, c), F32)],
        compiler_params=_cparams("parallel", "arbitrary"),
        name="dn_conv_prep",
    )(proj3, proj3, state, w)


def _scan_rows(x, row, limit):
    d = 1
    while d < limit:
        x = x + jnp.where(row >= d, pltpu.roll(x, d, axis=0), 0.0)
        d *= 2
    return x


def _gates_kernel(x_ref, bias_ref, alog_ref, gates_ref, cum_ref, gcl_ref, carry_ref, *, tt, chunk):
    t = pl.program_id(1)

    @pl.when(t == 0)
    def _():
        carry_ref[...] = jnp.zeros_like(carry_ref)

    x = x_ref[0] + bias_ref[...]
    lane = lax.broadcasted_iota(jnp.int32, x.shape, 1)
    row = lax.broadcasted_iota(jnp.int32, x.shape, 0)
    lp = jnp.log1p(jnp.exp(-jnp.abs(x)))
    softplus = jnp.maximum(x, 0.0) + lp
    logsig = jnp.minimum(x, 0.0) - lp
    decay = -jnp.exp(alog_ref[...]) * softplus
    out = jnp.where(lane < H_DN, _sigmoid(x),
                    jnp.where(lane < 2 * H_DN, decay,
                              jnp.where(lane < 2 * H_DN + H_FOX, logsig, 0.0)))
    gates_ref[0] = out
    c = _scan_rows(out, row, tt) + carry_ref[...]
    cum_ref[0] = c
    carry_ref[...] = c[tt - 1:tt, :]
    gcl_ref[0] = _scan_rows(out, row % chunk, min(chunk, tt))


def _gates(proj3, bias, alog, *, tt):
    b, t, _ = proj3.shape
    spec = pl.BlockSpec((1, tt, LANE), lambda i, j: (i, j, 0))
    return pl.pallas_call(
        functools.partial(_gates_kernel, tt=tt, chunk=DN_CHUNK),
        out_shape=[jax.ShapeDtypeStruct((b, t, LANE), F32)] * 3,
        grid=(b, t // tt),
        in_specs=[
            pl.BlockSpec((1, tt, LANE), lambda i, j: (i, j, BLK_GATE)),
            pl.BlockSpec((1, LANE), lambda i, j: (0, 0)),
            pl.BlockSpec((1, LANE), lambda i, j: (0, 0)),
        ],
        out_specs=[spec, spec, spec],
        scratch_shapes=[pltpu.VMEM((1, LANE), F32)],
        compiler_params=_cparams("parallel", "arbitrary"),
        name="gate_act",
    )(proj3, bias, alog)


def _gdn_kernel(q_ref, k_ref, v_ref, z_ref, gates_ref, gcl_ref, gclt_ref, s0_ref, nw_ref, o_ref, s_ref, *, c, nck):
    @pl.when(pl.program_id(1) == 0)
    def _():
        s_ref[...] = s0_ref[...]

    ii = lax.broadcasted_iota(jnp.int32, (c, c), 0)
    jj = lax.broadcasted_iota(jnp.int32, (c, c), 1)
    lower = ii >= jj
    strict = ii > jj
    eye = jnp.where(ii == jj, 1.0, 0.0).astype(F32)
    n_sq = c.bit_length() - 2
    heads = range(H_DN)
    hsl = _head_slices(H_DN)
    pairs = [(ck, h) for ck in range(nck) for h in heads]
    rows = [slice(ck * c, (ck + 1) * c) for ck in range(nck)]
    q = {p: q_ref[0, rows[p[0]], hsl[p[1]]] for p in pairs}
    k = {p: k_ref[0, rows[p[0]], hsl[p[1]]] for p in pairs}
    v = {p: v_ref[0, rows[p[0]], hsl[p[1]]] for p in pairs}
    beta = {p: gates_ref[0, rows[p[0]], p[1]:p[1] + 1] for p in pairs}
    gc = {p: gcl_ref[0, rows[p[0]], H_DN + p[1]:H_DN + p[1] + 1] for p in pairs}
    gr = {p: gclt_ref[0, p[0], p[1]:p[1] + 1, :] for p in pairs}
    gl = {p: gr[p][:, c - 1:c] for p in pairs}
    gamma = {p: jnp.exp(jnp.where(lower, gc[p] - gr[p], -jnp.inf)) for p in pairs}
    kb = {p: k[p] * beta[p] for p in pairs}
    kbf = {p: k[p].astype(BF16) for p in pairs}
    kk = {p: _dot_nt(kb[p].astype(BF16), kbf[p]) for p in pairs}
    mm = {p: jnp.where(strict, -(kk[p] * gamma[p]), 0.0) for p in pairs}
    tinv = {p: eye + mm[p] for p in pairs}
    ps = {p: _split(mm[p]) for p in pairs}
    for _ in range(n_sq):
        ps = {p: _split(_dot3(ps[p], ps[p])) for p in pairs}
        tinv = {p: tinv[p] + _dot3(_split(tinv[p]), ps[p]) for p in pairs}
    egc = {p: jnp.exp(gc[p]) for p in pairs}
    rhs = {p: _split(jnp.concatenate([kb[p] * egc[p], v[p] * beta[p]], axis=1)) for p in pairs}
    wu = {p: _dot3(_split(tinv[p]), rhs[p]) for p in pairs}
    qk = {p: (_dot_nt(q[p].astype(BF16), kbf[p]) * gamma[p]).astype(BF16) for p in pairs}
    wb = {p: wu[p][:, :HEAD_DIM].astype(BF16) for p in pairs}
    qg = {p: (q[p] * egc[p]).astype(BF16) for p in pairs}
    k_dec = {p: (k[p] * jnp.exp(gl[p] - gc[p])).astype(BF16) for p in pairs}
    s = [s_ref[0, h] for h in heads]
    for ck in range(nck):
        sb = [x.astype(BF16) for x in s]
        v_new = [wu[ck, h][:, HEAD_DIM:] - _dot(wb[ck, h], sb[h]) for h in heads]
        vnb = [x.astype(BF16) for x in v_new]
        s = [s[h] * jnp.exp(gl[ck, h]) + _dot_tn(k_dec[ck, h], vnb[h]) for h in heads]
        for h in heads:
            o = _dot(qg[ck, h], sb[h]) + _dot(qk[ck, h], vnb[h])
            o_ref[0, rows[ck], hsl[h]] = (_rms(o, nw_ref[...]) * _silu(z_ref[0, rows[ck], hsl[h]])).astype(o_ref.dtype)
    for h in heads:
        s_ref[0, h] = s[h]


def _gdn(q, k, v, z, z_blk, gates, gcl, gclt, s0, nw, *, nck):
    b, t, _ = q.shape
    c = DN_CHUNK
    tc = nck * c
    blk = pl.BlockSpec((1, tc, DN_W), lambda i, j: (i, j, 0))
    gblk = pl.BlockSpec((1, tc, LANE), lambda i, j: (i, j, 0))
    sblk = pl.BlockSpec((1, H_DN, HEAD_DIM, HEAD_DIM), lambda i, j: (i, 0, 0, 0))
    return pl.pallas_call(
        functools.partial(_gdn_kernel, c=c, nck=nck),
        out_shape=[jax.ShapeDtypeStruct((b, t, DN_W), BF16), jax.ShapeDtypeStruct(s0.shape, F32)],
        grid=(b, t // tc),
        in_specs=[
            blk, blk, blk,
            pl.BlockSpec((1, tc, DN_W), lambda i, j: (i, j, z_blk)),
            gblk, gblk,
            pl.BlockSpec((1, nck, 8, c), lambda i, j: (i, j, 0, 0)),
            sblk,
            pl.BlockSpec((1, HEAD_DIM), lambda i, j: (0, 0)),
        ],
        out_specs=[blk, sblk],
        compiler_params=_cparams("parallel", "arbitrary"),
        name="gated_delta_rule",
    )(q, k, v, z, gates, gcl, gclt, s0, nw)


def _fox_kernel(q_ref, k_ref, v_ref, ct_ref, nw_ref, o_ref, qb_ref, m_ref, l_ref, acc_ref, *, tq):
    qi = pl.program_id(1)
    heads = range(H_FOX)
    hsl = _head_slices(H_FOX)
    row = lax.broadcasted_iota(jnp.int32, (tq, tq), 0)
    col = lax.broadcasted_iota(jnp.int32, (tq, tq), 1)
    ones = jnp.ones((tq, HEAD_DIM), BF16)
    qb_ref[...] = q_ref[0].astype(BF16)
    m_ref[...] = jnp.full_like(m_ref, -jnp.inf)
    l_ref[...] = jnp.zeros_like(l_ref)
    acc_ref[...] = jnp.zeros_like(acc_ref)

    def step(j, diagonal):
        ks = pl.ds(pl.multiple_of(j * tq, tq), tq)
        s = [_dot_nt(qb_ref[:, hsl[h]], k_ref[0, ks, hsl[h]].astype(BF16)) * SCALE - ct_ref[0, j, h:h + 1, :]
             for h in heads]
        if diagonal:
            s = [jnp.where(col <= row, x, -jnp.inf) for x in s]
        m_old = [m_ref[h] for h in heads]
        m_new = [jnp.maximum(m_old[h], jnp.max(s[h], axis=-1, keepdims=True)) for h in heads]
        p = [jnp.exp(s[h] - m_new[h]) for h in heads]
        a = [jnp.exp(m_old[h] - m_new[h]) for h in heads]
        for h in heads:
            pb = p[h].astype(BF16)
            m_ref[h] = m_new[h]
            l_ref[h] = a[h] * l_ref[h] + _dot(pb, ones)
            acc_ref[h] = a[h] * acc_ref[h] + _dot(pb, v_ref[0, ks, hsl[h]].astype(BF16))

    def body(j, carry):
        step(j, False)
        return carry

    lax.fori_loop(0, qi, body, 0)
    step(qi, True)
    for h in heads:
        o_ref[0, :, hsl[h]] = _rms(acc_ref[h] / l_ref[h], nw_ref[...]).astype(o_ref.dtype)


def _fox_prompt(proj3, ct, nw, *, tq):
    b, t, _ = proj3.shape
    return pl.pallas_call(
        functools.partial(_fox_kernel, tq=tq),
        out_shape=jax.ShapeDtypeStruct((b, t, FOX_W), BF16),
        grid=(b, t // tq),
        in_specs=[
            pl.BlockSpec((1, tq, FOX_W), lambda i, j: (i, j, BLK_FQ // H_FOX)),
            pl.BlockSpec((1, t, FOX_W), lambda i, j: (i, 0, BLK_FK // H_FOX)),
            pl.BlockSpec((1, t, FOX_W), lambda i, j: (i, 0, BLK_FV // H_FOX)),
            pl.BlockSpec((1, t // tq, 8, tq), lambda i, j: (i, 0, 0, 0)),
            pl.BlockSpec((1, HEAD_DIM), lambda i, j: (0, 0)),
        ],
        out_specs=pl.BlockSpec((1, tq, FOX_W), lambda i, j: (i, j, 0)),
        scratch_shapes=[
            pltpu.VMEM((tq, FOX_W), BF16),
            pltpu.VMEM((H_FOX, tq, 1), F32),
            pltpu.VMEM((H_FOX, tq, HEAD_DIM), F32),
            pltpu.VMEM((H_FOX, tq, HEAD_DIM), F32),
        ],
        compiler_params=_cparams("parallel", "arbitrary"),
        name="fox_prompt",
    )(proj3, proj3, proj3, ct, nw)


def _suffix_incl(lg, tri):
    hi, lo = _split(lg)
    return _dot(hi, tri) + _dot(lo, tri)


def _sb_kernel(q_ref, k_ref, v_ref, nw_ref, o_ref, qb_ref, r_ref, acc_ref, *, tq):
    qi = pl.program_id(1)
    heads = range(H_SB)
    hsl = _head_slices(H_SB)
    row = lax.broadcasted_iota(jnp.int32, (tq, tq), 0)
    col = lax.broadcasted_iota(jnp.int32, (tq, tq), 1)
    tri = jnp.where(row >= col, 1.0, 0.0).astype(BF16)
    qb_ref[...] = q_ref[0].astype(BF16)
    r_ref[...] = jnp.zeros_like(r_ref)
    acc_ref[...] = jnp.zeros_like(acc_ref)

    def step(jr, diagonal):
        ks = pl.ds(pl.multiple_of((qi - jr) * tq, tq), tq)
        z = [_dot_nt(qb_ref[:, hsl[h]], k_ref[0, ks, hsl[h]].astype(BF16)) * SCALE for h in heads]
        lg = [_neg_softplus(x) for x in z]
        if diagonal:
            lg = [jnp.where(col < row, x, 0.0) for x in lg]
        incl = [_suffix_incl(x, tri) for x in lg]
        a = [jnp.exp(z[h] + incl[h] + r_ref[h]) for h in heads]
        if diagonal:
            a = [jnp.where(col < row, x, 0.0) for x in a]
        for h in heads:
            acc_ref[h] += _dot(a[h].astype(BF16), v_ref[0, ks, hsl[h]].astype(BF16))
            r_ref[h] += incl[h][:, 0:1]

    def body(jr, carry):
        step(jr, False)
        return carry

    step(0, True)
    lax.fori_loop(1, qi + 1, body, 0)
    for h in heads:
        o_ref[0, :, hsl[h]] = _rms(acc_ref[h], nw_ref[...]).astype(o_ref.dtype)


def _sb_prompt(proj3, nw, *, tq):
    b, t, _ = proj3.shape
    return pl.pallas_call(
        functools.partial(_sb_kernel, tq=tq),
        out_shape=jax.ShapeDtypeStruct((b, t, SB_W), BF16),
        grid=(b, t // tq),
        in_specs=[
            pl.BlockSpec((1, tq, SB_W), lambda i, j: (i, j, BLK_SQ // H_SB)),
            pl.BlockSpec((1, t, SB_W), lambda i, j: (i, 0, BLK_SK // H_SB)),
            pl.BlockSpec((1, t, SB_W), lambda i, j: (i, 0, BLK_SV // H_SB)),
            pl.BlockSpec((1, HEAD_DIM), lambda i, j: (0, 0)),
        ],
        out_specs=pl.BlockSpec((1, tq, SB_W), lambda i, j: (i, j, 0)),
        scratch_shapes=[
            pltpu.VMEM((tq, SB_W), BF16),
            pltpu.VMEM((H_SB, tq, 1), F32),
            pltpu.VMEM((H_SB, tq, HEAD_DIM), F32),
        ],
        compiler_params=_cparams("parallel", "arbitrary"),
        name="sb_prompt",
    )(proj3, proj3, proj3, nw)


def _page_rows(page_ref, head_major=False):
    x = page_ref[...] if head_major else jnp.swapaxes(page_ref[...], 0, 1)
    return jnp.concatenate([x[h] for h in range(x.shape[0])], axis=1).astype(BF16)


def _block_diag_q(q_ref, nh, t):
    zero = jnp.zeros((t, HEAD_DIM), F32)
    rows = [jnp.concatenate([q_ref[0, :, h * HEAD_DIM:(h + 1) * HEAD_DIM] if j == h else zero
                             for j in range(nh)], axis=1) for h in range(nh)]
    return jnp.concatenate(rows, axis=0).astype(BF16)


def _rows_per_head(x, nh, t):
    return jnp.concatenate([jnp.broadcast_to(x[h:h + 1, :], (t, x.shape[1])) for h in range(nh)], axis=0)


def _scores(qblk, keys):
    return _dot_nt(qblk, keys) * SCALE


def _weighted_values(w, values, nh, t):
    full = _dot(w.astype(BF16), values)
    return jnp.concatenate([full[h * t:(h + 1) * t, h * HEAD_DIM:(h + 1) * HEAD_DIM] for h in range(nh)], axis=0)


def _fox_dec_kernel(pt_ref, q_ref, kn_ref, vn_ref, cn_ref, nw_ref, *rest, g, t):
    kp, vp, lp = rest[:g], rest[g:2 * g], rest[2 * g:3 * g]
    o_ref = rest[3 * g]
    m_ref, l_ref, acc_ref, off_ref, kpad_ref, vpad_ref = rest[3 * g + 1:]
    nh = H_FOX
    hsl = _head_slices(nh)
    p = pl.program_id(1)

    @pl.when(p == 0)
    def _():
        m_ref[...] = jnp.full_like(m_ref, -jnp.inf)
        l_ref[...] = jnp.zeros_like(l_ref)
        acc_ref[...] = jnp.zeros_like(acc_ref)
        off_ref[...] = jnp.zeros_like(off_ref)

    qblk = _block_diag_q(q_ref, nh, t)
    ii = lax.broadcasted_iota(jnp.int32, (PAGE, PAGE), 0)
    jj = lax.broadcasted_iota(jnp.int32, (PAGE, PAGE), 1)
    upper = jnp.where(ii <= jj, 1.0, 0.0).astype(F32)

    def update(s, weighted_values):
        m = m_ref[...]
        m_new = jnp.maximum(m, jnp.max(s, axis=-1, keepdims=True))
        pr = jnp.exp(s - m_new)
        a = jnp.exp(m - m_new)
        l_ref[...] = a * l_ref[...] + jnp.sum(pr, axis=-1, keepdims=True)
        acc_ref[...] = a * acc_ref[...] + weighted_values(pr)
        m_ref[...] = m_new

    cw = [_dot_hi(lp[i][...], upper) for i in range(g)]
    off = off_ref[...]
    s_pages = []
    for i in range(g):
        s_pages.append(_scores(qblk, _page_rows(kp[i], True)) - _rows_per_head(off + cw[i], nh, t))
        off = off + cw[i][:, PAGE - 1:PAGE]
    off_ref[...] = off
    values = jnp.concatenate([_page_rows(vp[i], True) for i in range(g)], axis=0)
    update(jnp.concatenate(s_pages, axis=1), lambda pr: _weighted_values(pr, values, nh, t))

    @pl.when(p == pl.num_programs(1) - 1)
    def _():
        kpad_ref[...] = jnp.zeros_like(kpad_ref)
        vpad_ref[...] = jnp.zeros_like(vpad_ref)
        kpad_ref[0:t, :] = kn_ref[0]
        vpad_ref[0:t, :] = vn_ref[0]
        c = off_ref[...] + cn_ref[0]
        s = _scores(qblk, kpad_ref[...].astype(BF16)) - _rows_per_head(c, nh, t)
        r = lax.broadcasted_iota(jnp.int32, s.shape, 0)
        cc = lax.broadcasted_iota(jnp.int32, s.shape, 1)
        s = jnp.where(cc <= r % t, s, -jnp.inf)
        update(s, lambda pr: _weighted_values(pr, vpad_ref[...].astype(BF16), nh, t))
        o = acc_ref[...] / l_ref[...]
        for h in range(nh):
            o_ref[0, :, hsl[h]] = _rms(o[h * t:(h + 1) * t], nw_ref[...]).astype(o_ref.dtype)


def _fox_dec(page_table, proj3, cnt, nw, k_pool, v_pool, lf_pool, layer, *, g):
    b, t, _ = proj3.shape
    npg = page_table.shape[1]
    nh = H_FOX

    def page_spec(shape, i):
        return pl.BlockSpec((None, None) + shape, lambda bi, p, pt, i=i: (layer, pt[bi, p * g + i], 0, 0, 0)[:2 + len(shape)])

    in_specs = [
        pl.BlockSpec((1, t, FOX_W), lambda bi, p, pt: (bi, 0, BLK_FQ // nh)),
        pl.BlockSpec((1, t, FOX_W), lambda bi, p, pt: (bi, 0, BLK_FK // nh)),
        pl.BlockSpec((1, t, FOX_W), lambda bi, p, pt: (bi, 0, BLK_FV // nh)),
        pl.BlockSpec((1, 8, PAGE), lambda bi, p, pt: (bi, 0, 0)),
        pl.BlockSpec((1, HEAD_DIM), lambda bi, p, pt: (0, 0)),
    ]
    in_specs += [page_spec((nh, PAGE, HEAD_DIM), i) for i in range(g)]
    in_specs += [page_spec((nh, PAGE, HEAD_DIM), i) for i in range(g)]
    in_specs += [page_spec((8, PAGE), i) for i in range(g)]
    grid_spec = pltpu.PrefetchScalarGridSpec(
        num_scalar_prefetch=1,
        grid=(b, npg // g),
        in_specs=in_specs,
        out_specs=pl.BlockSpec((1, t, FOX_W), lambda bi, p, pt: (bi, 0, 0)),
        scratch_shapes=[
            pltpu.VMEM((nh * t, 1), F32),
            pltpu.VMEM((nh * t, 1), F32),
            pltpu.VMEM((nh * t, HEAD_DIM), F32),
            pltpu.VMEM((8, 1), F32),
            pltpu.VMEM((PAGE, FOX_W), F32),
            pltpu.VMEM((PAGE, FOX_W), F32),
        ],
    )
    return pl.pallas_call(
        functools.partial(_fox_dec_kernel, g=g, t=t),
        out_shape=jax.ShapeDtypeStruct((b, t, FOX_W), BF16),
        grid_spec=grid_spec,
        compiler_params=_cparams("parallel", "arbitrary"),
        name="fox_paged",
    )(page_table, proj3, proj3, proj3, cnt, nw, *([k_pool] * g), *([v_pool] * g), *([lf_pool] * g))


def _sb_dec_kernel(pt_ref, q_ref, kn_ref, vn_ref, nw_ref, *rest, g, t):
    kp, vp = rest[:g], rest[g:2 * g]
    o_ref = rest[2 * g]
    rsum_ref, acc_ref, kpad_ref, vpad_ref = rest[2 * g + 1:]
    nh = H_SB
    hsl = _head_slices(nh)
    p = pl.program_id(1)
    ii = lax.broadcasted_iota(jnp.int32, (PAGE, PAGE), 0)
    jj = lax.broadcasted_iota(jnp.int32, (PAGE, PAGE), 1)
    tri = jnp.where(ii >= jj, 1.0, 0.0).astype(BF16)
    qblk = _block_diag_q(q_ref, nh, t)

    @pl.when(p == 0)
    def _():
        kpad_ref[...] = jnp.zeros_like(kpad_ref)
        vpad_ref[...] = jnp.zeros_like(vpad_ref)
        kpad_ref[0:t, :] = kn_ref[0]
        vpad_ref[0:t, :] = vn_ref[0]
        z = _scores(qblk, kpad_ref[...].astype(BF16))
        r = lax.broadcasted_iota(jnp.int32, z.shape, 0)
        cc = lax.broadcasted_iota(jnp.int32, z.shape, 1)
        valid = cc < r % t
        incl = _suffix_incl(jnp.where(valid, _neg_softplus(z), 0.0), tri)
        a = jnp.where(valid, jnp.exp(z + incl), 0.0)
        rsum_ref[...] = incl[:, 0:1]
        acc_ref[...] = _weighted_values(a, vpad_ref[...].astype(BF16), nh, t)

    zs, incls = [], []
    for i in range(g):
        z = _scores(qblk, _page_rows(kp[i]))
        zs.append(z)
        incls.append(_suffix_incl(_neg_softplus(z), tri))
    rsum = rsum_ref[...]
    weights = []
    for i in range(g):
        weights.append(jnp.exp(zs[i] + incls[i] + rsum))
        rsum = rsum + incls[i][:, 0:1]
    rsum_ref[...] = rsum
    values = jnp.concatenate([_page_rows(vp[i]) for i in range(g)], axis=0)
    acc_ref[...] += _weighted_values(jnp.concatenate(weights, axis=1), values, nh, t)

    @pl.when(p == pl.num_programs(1) - 1)
    def _():
        acc = acc_ref[...]
        for h in range(nh):
            o_ref[0, :, hsl[h]] = _rms(acc[h * t:(h + 1) * t], nw_ref[...]).astype(o_ref.dtype)


def _sb_dec(page_table, proj3, nw, k_pool, v_pool, layer, *, g):
    b, t, _ = proj3.shape
    npg = page_table.shape[1]
    nh = H_SB

    def page_spec(i):
        return pl.BlockSpec((None, None, PAGE, nh, HEAD_DIM),
                            lambda bi, p, pt, i=i: (layer, pt[bi, npg - 1 - (p * g + i)], 0, 0, 0))

    in_specs = [
        pl.BlockSpec((1, t, SB_W), lambda bi, p, pt: (bi, 0, BLK_SQ // nh)),
        pl.BlockSpec((1, t, SB_W), lambda bi, p, pt: (bi, 0, BLK_SK // nh)),
        pl.BlockSpec((1, t, SB_W), lambda bi, p, pt: (bi, 0, BLK_SV // nh)),
        pl.BlockSpec((1, HEAD_DIM), lambda bi, p, pt: (0, 0)),
    ]
    in_specs += [page_spec(i) for i in range(g)] * 2
    grid_spec = pltpu.PrefetchScalarGridSpec(
        num_scalar_prefetch=1,
        grid=(b, npg // g),
        in_specs=in_specs,
        out_specs=pl.BlockSpec((1, t, SB_W), lambda bi, p, pt: (bi, 0, 0)),
        scratch_shapes=[
            pltpu.VMEM((nh * t, 1), F32),
            pltpu.VMEM((nh * t, HEAD_DIM), F32),
            pltpu.VMEM((PAGE, SB_W), F32),
            pltpu.VMEM((PAGE, SB_W), F32),
        ],
    )
    return pl.pallas_call(
        functools.partial(_sb_dec_kernel, g=g, t=t),
        out_shape=jax.ShapeDtypeStruct((b, t, SB_W), BF16),
        grid_spec=grid_spec,
        compiler_params=_cparams("parallel", "arbitrary"),
        name="sb_paged",
    )(page_table, proj3, proj3, proj3, nw, *([k_pool] * g), *([v_pool] * g))


def _head_major_kernel(*refs, nh):
    x_ref, o_ref = refs[0], refs[-1]
    for h in range(nh):
        o_ref[0, h] = x_ref[0, :, h * HEAD_DIM:(h + 1) * HEAD_DIM]


def _head_major_rows(proj3, col_blk, nh, layer, depth, prev, *, tt):
    b, t, _ = proj3.shape
    w = nh * HEAD_DIM
    in_specs = [pl.BlockSpec((1, tt, w), lambda i, j: (i, j, col_blk * LANE // w))]
    args = [proj3]
    aliases = {}
    if prev is not None:
        in_specs.append(pl.BlockSpec(memory_space=pl.ANY))
        args.append(prev)
        aliases = {1: 0}
    return pl.pallas_call(
        functools.partial(_head_major_kernel, nh=nh),
        out_shape=jax.ShapeDtypeStruct((depth, b, nh, t, HEAD_DIM), F32),
        grid=(b, t // tt),
        in_specs=in_specs,
        out_specs=pl.BlockSpec((None, 1, nh, tt, HEAD_DIM), lambda i, j: (layer, i, 0, j, 0)),
        input_output_aliases=aliases,
        compiler_params=_cparams("parallel", "parallel"),
        name="head_major_rows",
    )(*args)


def _out_proj_kernel(x_ref, a_ref, b_ref, c_ref, w_ref, g_ref, o_ref):
    acc = _dot(a_ref[...], w_ref[0:DN_W, :])
    acc += _dot(b_ref[...], w_ref[DN_W:DN_W + FOX_W, :])
    acc += _dot(c_ref[...], w_ref[DN_W + FOX_W:, :])
    o_ref[...] = x_ref[...] + _rms(acc, g_ref[...])


def _out_proj(x, oa, ob, oc, w, g, *, tm):
    m, d = x.shape
    return pl.pallas_call(
        _out_proj_kernel,
        out_shape=jax.ShapeDtypeStruct((m, d), F32),
        grid=(m // tm,),
        in_specs=[
            pl.BlockSpec((tm, d), lambda i: (i, 0)),
            pl.BlockSpec((tm, DN_W), lambda i: (i, 0)),
            pl.BlockSpec((tm, FOX_W), lambda i: (i, 0)),
            pl.BlockSpec((tm, SB_W), lambda i: (i, 0)),
            pl.BlockSpec(w.shape, lambda i: (0, 0)),
            pl.BlockSpec((1, d), lambda i: (0, 0)),
        ],
        out_specs=pl.BlockSpec((tm, d), lambda i: (i, 0)),
        compiler_params=_cparams("parallel"),
        name="out_proj",
    )(x, oa, ob, oc, w, g)


def _ffn_tail(c, x_ref, gpost_ref, acc_ref, o_ref):
    @pl.when(c == pl.num_programs(1) - 1)
    def _():
        o_ref[...] = x_ref[...] + _rms(acc_ref[...], gpost_ref[...])


def _ffn_prompt_kernel(x_ref, halo_ref, gpre_ref, wg_ref, wu_ref, cg_ref, cu_ref, wd_ref, gpost_ref,
                       o_ref, st_ref, hn_ref, acc_ref, ug_ref, uu_ref, *, tm, tiles_per_seq):
    i = pl.program_id(0)
    c = pl.program_id(1)

    @pl.when(c == 0)
    def _():
        hn_ref[pl.ds(8, tm), :] = _rms(x_ref[...], gpre_ref[...]).astype(BF16)
        keep = jnp.where(i % tiles_per_seq == 0, 0.0, 1.0)
        hn_ref[0:8, :] = (_rms(halo_ref[...], gpre_ref[...]) * keep).astype(BF16)
        acc_ref[...] = jnp.zeros_like(acc_ref)

    hn = hn_ref[...]
    ug_ref[...] = _dot(hn, wg_ref[...])
    uu_ref[...] = _dot(hn, wu_ref[...])

    def conv(u_ref, w_ref):
        y = u_ref[pl.ds(8 - (FFN_CONV - 1), tm), :] * w_ref[0:1, :]
        for k in range(1, FFN_CONV):
            y = y + u_ref[pl.ds(8 - (FFN_CONV - 1) + k, tm), :] * w_ref[k:k + 1, :]
        return y

    act = _silu(conv(ug_ref, cg_ref)) * conv(uu_ref, cu_ref)
    acc_ref[...] += _dot(act.astype(BF16), wd_ref[...])
    st_ref[0, 0] = ug_ref[pl.ds(tm + 8 - (FFN_CONV - 1), FFN_CONV - 1), :]
    st_ref[0, 1] = uu_ref[pl.ds(tm + 8 - (FFN_CONV - 1), FFN_CONV - 1), :]
    _ffn_tail(c, x_ref, gpost_ref, acc_ref, o_ref)


def _ffn_prompt(x, seq_len, gpre, w_up, conv_w, w_down, gpost, *, tm, tc):
    m, d = x.shape
    dff = w_down.shape[0]
    nc = dff // tc
    tps = seq_len // tm
    hb = tm // 8
    return pl.pallas_call(
        functools.partial(_ffn_prompt_kernel, tm=tm, tiles_per_seq=tps),
        out_shape=[jax.ShapeDtypeStruct((m, d), F32),
                   jax.ShapeDtypeStruct((m // tm, 2, FFN_CONV - 1, dff), F32)],
        grid=(m // tm, nc),
        in_specs=[
            pl.BlockSpec((tm, d), lambda i, c: (i, 0)),
            pl.BlockSpec((8, d), lambda i, c: (jnp.maximum(i * hb - 1, 0), 0)),
            pl.BlockSpec((1, d), lambda i, c: (0, 0)),
            pl.BlockSpec((d, tc), lambda i, c: (0, c)),
            pl.BlockSpec((d, tc), lambda i, c: (0, nc + c)),
            pl.BlockSpec((FFN_CONV, tc), lambda i, c: (0, c)),
            pl.BlockSpec((FFN_CONV, tc), lambda i, c: (0, nc + c)),
            pl.BlockSpec((tc, d), lambda i, c: (c, 0)),
            pl.BlockSpec((1, d), lambda i, c: (0, 0)),
        ],
        out_specs=[
            pl.BlockSpec((tm, d), lambda i, c: (i, 0)),
            pl.BlockSpec((1, 2, FFN_CONV - 1, tc), lambda i, c: (i, 0, 0, c)),
        ],
        scratch_shapes=[
            pltpu.VMEM((tm + 8, d), BF16),
            pltpu.VMEM((tm, d), F32),
            pltpu.VMEM((tm + 8, tc), F32),
            pltpu.VMEM((tm + 8, tc), F32),
        ],
        compiler_params=_cparams("parallel", "arbitrary"),
        name="conv_ffn_prompt",
    )(x, x, gpre, w_up, w_up, conv_w, conv_w, w_down, gpost)


def _ffn_dec_kernel(x_ref, gpre_ref, wg_ref, wu_ref, cg_ref, cu_ref, wd_ref, gpost_ref,
                    s0g_ref, s1g_ref, s0u_ref, s1u_ref,
                    o_ref, ugo_ref, uuo_ref, hn_ref, acc_ref, ug_ref, uu_ref, *, tm, t):
    c = pl.program_id(1)

    @pl.when(c == 0)
    def _():
        hn_ref[...] = _rms(x_ref[...], gpre_ref[...]).astype(BF16)
        acc_ref[...] = jnp.zeros_like(acc_ref)
        ug_ref[0:8, :] = jnp.zeros((8, ug_ref.shape[1]), F32)
        uu_ref[0:8, :] = jnp.zeros((8, uu_ref.shape[1]), F32)

    hn = hn_ref[...]
    ug = _dot(hn, wg_ref[...])
    uu = _dot(hn, wu_ref[...])
    ugo_ref[...] = ug
    uuo_ref[...] = uu
    ug_ref[pl.ds(8, tm), :] = ug
    uu_ref[pl.ds(8, tm), :] = uu
    pos = lax.broadcasted_iota(jnp.int32, ug.shape, 0) % t

    def conv(u, u_ref, s0_ref, s1_ref, w_ref):
        um1 = jnp.where(pos >= 1, u_ref[pl.ds(7, tm), :], s1_ref[...])
        um2 = jnp.where(pos >= 2, u_ref[pl.ds(6, tm), :], jnp.where(pos == 0, s0_ref[...], s1_ref[...]))
        return um2 * w_ref[0:1, :] + um1 * w_ref[1:2, :] + u * w_ref[2:3, :]

    act = _silu(conv(ug, ug_ref, s0g_ref, s1g_ref, cg_ref)) * conv(uu, uu_ref, s0u_ref, s1u_ref, cu_ref)
    acc_ref[...] += _dot(act.astype(BF16), wd_ref[...])
    _ffn_tail(c, x_ref, gpost_ref, acc_ref, o_ref)


def _ffn_dec(x, t, gpre, w_up, conv_w, w_down, gpost, s0e, s1e, *, tc):
    m, d = x.shape
    dff = w_down.shape[0]
    nc = dff // tc
    tm = m
    st = lambda off: pl.BlockSpec((tm, tc), lambda i, c, off=off: (0, off * nc + c))
    return pl.pallas_call(
        functools.partial(_ffn_dec_kernel, tm=tm, t=t),
        out_shape=[jax.ShapeDtypeStruct((m, d), F32),
                   jax.ShapeDtypeStruct((m, dff), F32), jax.ShapeDtypeStruct((m, dff), F32)],
        grid=(1, nc),
        in_specs=[
            pl.BlockSpec((tm, d), lambda i, c: (0, 0)),
            pl.BlockSpec((1, d), lambda i, c: (0, 0)),
            pl.BlockSpec((d, tc), lambda i, c: (0, c)),
            pl.BlockSpec((d, tc), lambda i, c: (0, nc + c)),
            pl.BlockSpec((FFN_CONV, tc), lambda i, c: (0, c)),
            pl.BlockSpec((FFN_CONV, tc), lambda i, c: (0, nc + c)),
            pl.BlockSpec((tc, d), lambda i, c: (c, 0)),
            pl.BlockSpec((1, d), lambda i, c: (0, 0)),
            st(0), st(0), st(1), st(1),
        ],
        out_specs=[
            pl.BlockSpec((tm, d), lambda i, c: (0, 0)),
            pl.BlockSpec((tm, tc), lambda i, c: (0, c)),
            pl.BlockSpec((tm, tc), lambda i, c: (0, c)),
        ],
        scratch_shapes=[
            pltpu.VMEM((tm, d), BF16),
            pltpu.VMEM((tm, d), F32),
            pltpu.VMEM((tm + 8, tc), F32),
            pltpu.VMEM((tm + 8, tc), F32),
        ],
        compiler_params=_cparams("arbitrary", "arbitrary"),
        name="conv_ffn_sample",
    )(x, gpre, w_up, w_up, conv_w, conv_w, w_down, gpost, s0e, s1e, s0e, s1e)


def _pack_w_in(w_in):
    d = w_in.shape[0]
    wt = w_in.T
    sizes = (3 * DN_W, H_DN, H_DN, DN_W, FOX_W, FOX_W, FOX_W, H_FOX, SB_W, SB_W, SB_W)
    parts, start = [], 0
    for s in sizes:
        parts.append(wt[start:start + s])
        start += s
    qkv, dn_b, dn_a, dn_z, fq, fk, fv, ff, sq, sk, sv = parts
    gate = jnp.concatenate([dn_b, dn_a, ff, jnp.zeros((2 * LANE - 2 * H_DN - H_FOX, d), w_in.dtype)], axis=0)
    return jnp.concatenate([qkv, dn_z, fq, fk, fv, gate, sq, sk, sv], axis=0).astype(BF16)


def _lane_row(pairs):
    row = jnp.zeros((LANE,), F32)
    for off, v in pairs:
        row = row.at[off:off + v.shape[0]].set(v.astype(F32))
    return row[None, :]


def _mixers_common(x3, lw, dn_state, tm, tt):
    b, t, d = x3.shape
    proj = _norm_matmul(x3.reshape(b * t, d), lw["norm_pre_mix"], lw["w_in"], tm=tm, tn=1024)
    proj3 = proj.reshape(b, t, PACK_W)
    q, k, v, dn_buf_new = _dn_pre(proj3, dn_state, lw["dn_conv"], tt=tt)
    gates, cum, gcl = _gates(proj3, lw["gate_bias"], lw["alog_row"], tt=tt)
    return proj3, q, k, v, dn_buf_new, gates, cum, gcl


def _gclt(gcl, c):
    b, t, _ = gcl.shape
    g = gcl[:, :, H_DN:2 * H_DN].reshape(b, t // c, c, H_DN)
    g = jnp.swapaxes(g, 2, 3)
    return jnp.pad(g, ((0, 0), (0, 0), (0, 8 - H_DN), (0, 0)))


def _new_rows(proj3, gates):
    b, t, _ = proj3.shape

    def cols(blk, w, nh):
        return proj3[:, :, blk * LANE:blk * LANE + w].reshape(b, t, nh, HEAD_DIM)

    return (cols(BLK_FK, FOX_W, H_FOX), cols(BLK_FV, FOX_W, H_FOX), gates[:, :, 2 * H_DN:2 * H_DN + H_FOX],
            cols(BLK_SK, SB_W, H_SB), cols(BLK_SV, SB_W, H_SB))


def _layer_prompt(x3, lw, layer, depth, fox_bufs):
    b, t, d = x3.shape
    tq = 256
    tm_ffn = 512
    zeros_dn = jnp.zeros((b, DN_CONV - 1, 3 * DN_W), F32)
    proj3, q, k, v, dn_buf_new, gates, cum, gcl = _mixers_common(x3, lw, zeros_dn, tm=1024, tt=256)
    s0 = jnp.zeros((b, H_DN, HEAD_DIM, HEAD_DIM), F32)
    o_a, s_new = _gdn(q, k, v, proj3, BLK_Z // H_DN, gates, gcl, _gclt(gcl, DN_CHUNK), s0, lw["dn_norm"],
                      nck=GDN_CHUNKS_PER_STEP)
    ct = cum[:, :, 2 * H_DN:2 * H_DN + H_FOX].reshape(b, t // tq, tq, H_FOX)
    ct = jnp.pad(jnp.swapaxes(ct, 2, 3), ((0, 0), (0, 0), (0, 8 - H_FOX), (0, 0)))
    o_b = _fox_prompt(proj3, ct, lw["fox_norm"], tq=tq)
    o_c = _sb_prompt(proj3, lw["sb_norm"], tq=tq)
    m = b * t
    x1 = _out_proj(x3.reshape(m, d), o_a.reshape(m, DN_W), o_b.reshape(m, FOX_W), o_c.reshape(m, SB_W),
                   lw["w_out"], lw["norm_post_mix"], tm=512)
    y, st = _ffn_prompt(x1, t, lw["norm_pre_ffn"], lw["ffn_w_up"], lw["ffn_conv"], lw["ffn_w_down"],
                        lw["norm_post_ffn"], tm=tm_ffn, tc=512)
    tps = t // tm_ffn
    ffn_buf_new = jnp.swapaxes(st[tps - 1::tps], 1, 2).reshape(b, FFN_CONV - 1, -1)
    _, _, logf, sk, sv = _new_rows(proj3, gates)
    fox_bufs = (_head_major_rows(proj3, BLK_FK, H_FOX, layer, depth, fox_bufs[0], tt=512),
                _head_major_rows(proj3, BLK_FV, H_FOX, layer, depth, fox_bufs[1], tt=512))
    return y.reshape(b, t, d), (s_new, dn_buf_new, logf, sk, sv, ffn_buf_new), fox_bufs


def _layer_sample(x3, lw, dn_s0, dn_buf, ffn_buf, pools, page_table, layer):
    b, t, d = x3.shape
    c = DN_CHUNK
    fox_k, fox_v, fox_lf, sb_k, sb_v = pools
    proj3, q, k, v, dn_buf_new, gates, cum, gcl = _mixers_common(x3, lw, dn_buf, tm=b * t, tt=t)
    padt = ((0, 0), (0, c - t), (0, 0))
    z = proj3[:, :, BLK_Z * LANE:BLK_Z * LANE + DN_W]
    gcl_pad = jnp.pad(gcl, padt, mode="edge")
    o_a, s_new = _gdn(jnp.pad(q, padt), jnp.pad(k, padt), jnp.pad(v, padt), jnp.pad(z, padt), 0,
                      jnp.pad(gates, padt), gcl_pad, _gclt(gcl_pad, c), dn_s0, lw["dn_norm"], nck=1)
    o_a = o_a[:, :t]
    cnt = jnp.swapaxes(cum[:, :, 2 * H_DN:2 * H_DN + H_FOX], 1, 2)
    cnt = jnp.pad(cnt, ((0, 0), (0, 8 - H_FOX), (0, PAGE - t)))
    o_b = _fox_dec(page_table, proj3, cnt, lw["fox_norm"], fox_k, fox_v, fox_lf, layer, g=PAGES_PER_STEP)
    o_c = _sb_dec(page_table, proj3, lw["sb_norm"], sb_k, sb_v, layer, g=PAGES_PER_STEP)
    m = b * t
    x1 = _out_proj(x3.reshape(m, d), o_a.reshape(m, DN_W), o_b.reshape(m, FOX_W), o_c.reshape(m, SB_W),
                   lw["w_out"], lw["norm_post_mix"], tm=m)
    s0e = jnp.repeat(ffn_buf[:, 0, :], t, axis=0)
    s1e = jnp.repeat(ffn_buf[:, 1, :], t, axis=0)
    y, ug, uu = _ffn_dec(x1, t, lw["norm_pre_ffn"], lw["ffn_w_up"], lw["ffn_conv"], lw["ffn_w_down"],
                         lw["norm_post_ffn"], s0e, s1e, tc=512)
    u = jnp.concatenate([ug, uu], axis=1).reshape(b, t, -1)
    ffn_buf_new = u[:, t - (FFN_CONV - 1):]
    fk, fv, logf, sk, sv = _new_rows(proj3, gates)
    return y.reshape(b, t, d), (s_new, dn_buf_new, fk, fv, logf, sk, sv, ffn_buf_new)


def kernel(x_prompt, x_sample, state_dn_S, state_dn_conv, cache_fox_k, cache_fox_v, cache_fox_logf, cache_sb_k, cache_sb_v, state_ffn_conv, page_table, norm_pre_mix, w_in, dn_conv, dn_A_log, dn_dt_bias, dn_norm, fox_b_f, fox_norm, sb_norm, w_out, norm_post_mix, norm_pre_ffn, ffn_w_up, ffn_conv, ffn_w_down, norm_post_ffn):
    depth = w_in.shape[0]
    yp, ys = x_prompt, x_sample
    prompt_new, sample_new = [], []
    fox_bufs = (None, None)
    lf_t = jnp.pad(jnp.swapaxes(cache_fox_logf, 2, 3), ((0, 0), (0, 0), (0, 8 - H_FOX), (0, 0)))
    pools = (jnp.swapaxes(cache_fox_k, 2, 3), jnp.swapaxes(cache_fox_v, 2, 3), lf_t, cache_sb_k, cache_sb_v)
    for l in range(depth):
        lw = {
            "norm_pre_mix": norm_pre_mix[l][None, :],
            "w_in": _pack_w_in(w_in[l]),
            "dn_conv": dn_conv[l],
            "gate_bias": _lane_row([(H_DN, dn_dt_bias[l]), (2 * H_DN, fox_b_f[l])]),
            "alog_row": _lane_row([(H_DN, dn_A_log[l])]),
            "dn_norm": dn_norm[l][None, :],
            "fox_norm": fox_norm[l][None, :],
            "sb_norm": sb_norm[l][None, :],
            "w_out": w_out[l].astype(BF16),
            "norm_post_mix": norm_post_mix[l][None, :],
            "norm_pre_ffn": norm_pre_ffn[l][None, :],
            "ffn_w_up": ffn_w_up[l].astype(BF16),
            "ffn_conv": ffn_conv[l],
            "ffn_w_down": ffn_w_down[l].astype(BF16),
            "norm_post_ffn": norm_post_ffn[l][None, :],
        }
        yp, st_p, fox_bufs = _layer_prompt(yp, lw, l, depth, fox_bufs)
        prompt_new.append(st_p)
        ys, st_s = _layer_sample(ys, lw, state_dn_S[l], state_dn_conv[l], state_ffn_conv[l], pools, page_table, l)
        sample_new.append(st_s)
    dn_s_p, dn_conv_p, logf_p, sk_p, sv_p, ffn_p = [jnp.stack(a) for a in zip(*prompt_new)]
    fk_p, fv_p = [jnp.swapaxes(x, 2, 3) for x in fox_bufs]
    s_out = [jnp.stack(a) for a in zip(*sample_new)]
    return (yp, ys, dn_s_p, dn_conv_p, fk_p, fv_p, logf_p, sk_p, sv_p, ffn_p, *s_out)
```

```python
import functools

import jax
import jax.numpy as jnp
from jax import lax
from jax.experimental import pallas as pl
from jax.experimental.pallas import tpu as pltpu

F32 = jnp.float32
BF16 = jnp.bfloat16

HEAD_DIM = 128
H_DN = 6
H_FOX = 6
H_SB = 4
DN_W = H_DN * HEAD_DIM
FOX_W = H_FOX * HEAD_DIM
SB_W = H_SB * HEAD_DIM
DN_CONV = 4
FFN_CONV = 3
DN_CHUNK = 64
PAGE = 128
EPS = 1e-6
SCALE = HEAD_DIM ** -0.5
LANE = 128

BLK_QKV = 0
BLK_Z = 18
BLK_FQ = 24
BLK_FK = 30
BLK_FV = 36
BLK_GATE = 42
BLK_SQ = 44
BLK_SK = 48
BLK_SV = 52
N_BLK = 56
PACK_W = N_BLK * LANE

VMEM_LIMIT = 56 * 1024 * 1024
FOX_HEAD_GROUP = 6
GDN_CHUNKS_PER_STEP = 4
PAGES_PER_STEP = 16


def _cparams(*sem):
    return pltpu.CompilerParams(dimension_semantics=sem, vmem_limit_bytes=VMEM_LIMIT)


def _dot(a, b):
    return jnp.dot(a, b, preferred_element_type=F32)


def _dot_nt(a, b):
    return lax.dot_general(a, b, (((1,), (1,)), ((), ())), preferred_element_type=F32)


def _dot_tn(a, b):
    return lax.dot_general(a, b, (((0,), (0,)), ((), ())), preferred_element_type=F32)


def _dot_hi(a, b):
    return jnp.dot(a, b, preferred_element_type=F32, precision=lax.Precision.HIGHEST)


def _split(x):
    hi = x.astype(BF16)
    return hi, (x - hi.astype(F32)).astype(BF16)


def _dot3(a, b):
    return _dot(a[0], b[0]) + (_dot(a[0], b[1]) + _dot(a[1], b[0]))


def _sigmoid(x):
    return 1.0 / (1.0 + jnp.exp(-x))


def _silu(x):
    return x * _sigmoid(x)


def _neg_softplus(z):
    return -(jnp.maximum(z, 0.0) + jnp.log(1.0 + jnp.exp(-jnp.abs(z))))


def _rms(x, w):
    return x * lax.rsqrt(jnp.mean(x * x, axis=-1, keepdims=True) + EPS) * w


def _head_slices(n):
    return [slice(h * HEAD_DIM, (h + 1) * HEAD_DIM) for h in range(n)]


def _norm_mm_kernel(x_ref, g_ref, w_ref, o_ref, hn_ref):
    @pl.when(pl.program_id(1) == 0)
    def _():
        hn_ref[...] = _rms(x_ref[...], g_ref[...]).astype(BF16)

    o_ref[...] = _dot_nt(hn_ref[...], w_ref[...])


def _norm_matmul(x, g, wt, *, tm, tn):
    m, d = x.shape
    n = wt.shape[0]
    return pl.pallas_call(
        _norm_mm_kernel,
        out_shape=jax.ShapeDtypeStruct((m, n), F32),
        grid=(m // tm, n // tn),
        in_specs=[
            pl.BlockSpec((tm, d), lambda i, j: (i, 0)),
            pl.BlockSpec((1, d), lambda i, j: (0, 0)),
            pl.BlockSpec((tn, d), lambda i, j: (j, 0)),
        ],
        out_specs=pl.BlockSpec((tm, tn), lambda i, j: (i, j)),
        scratch_shapes=[pltpu.VMEM((tm, d), BF16)],
        compiler_params=_cparams("parallel", "arbitrary"),
        name="norm_in_proj",
    )(x, g, wt)


def _dn_pre_kernel(x_ref, halo_ref, st_ref, w_ref, q_ref, k_ref, v_ref, newst_ref, xc_ref, *, tt):
    t = pl.program_id(1)
    xc_ref[pl.ds(8, tt), :] = x_ref[0]

    @pl.when(t == 0)
    def _():
        xc_ref[5:8, :] = st_ref[0]

    @pl.when(t > 0)
    def _():
        xc_ref[0:8, :] = halo_ref[0]

    for part in range(3):
        for h in range(H_DN):
            cs = slice(part * DN_W + h * HEAD_DIM, part * DN_W + (h + 1) * HEAD_DIM)
            hs = slice(h * HEAD_DIM, (h + 1) * HEAD_DIM)
            y = xc_ref[pl.ds(5, tt), cs] * w_ref[0:1, cs]
            for i in range(1, DN_CONV):
                y = y + xc_ref[pl.ds(5 + i, tt), cs] * w_ref[i:i + 1, cs]
            y = _silu(y)
            if part == 2:
                v_ref[0, :, hs] = y
            else:
                yn = y * lax.rsqrt(jnp.sum(y * y, axis=-1, keepdims=True) + EPS)
                if part == 0:
                    q_ref[0, :, hs] = yn * SCALE
                else:
                    k_ref[0, :, hs] = yn
    newst_ref[0] = xc_ref[pl.ds(tt + 5, DN_CONV - 1), :]


def _dn_pre(proj3, state, w, *, tt):
    b, t, _ = proj3.shape
    c = 3 * DN_W
    hb = tt // 8
    out_shape = [jax.ShapeDtypeStruct((b, t, DN_W), F32)] * 3 + [jax.ShapeDtypeStruct((b, DN_CONV - 1, c), F32)]
    qkv_spec = pl.BlockSpec((1, tt, DN_W), lambda i, j: (i, j, 0))
    return pl.pallas_call(
        functools.partial(_dn_pre_kernel, tt=tt),
        out_shape=out_shape,
        grid=(b, t // tt),
        in_specs=[
            pl.BlockSpec((1, tt, c), lambda i, j: (i, j, 0)),
            pl.BlockSpec((1, 8, c), lambda i, j: (i, jnp.maximum(j * hb - 1, 0), 0)),
            pl.BlockSpec((1, DN_CONV - 1, c), lambda i, j: (i, 0, 0)),
            pl.BlockSpec((DN_CONV, c), lambda i, j: (0, 0)),
        ],
        out_specs=[qkv_spec, qkv_spec, qkv_spec, pl.BlockSpec((1, DN_CONV - 1, c), lambda i, j: (i, 0, 0))],
        scratch_shapes=[pltpu.VMEM((tt + 8, c), F32)],
        compiler_params=_cparams("parallel", "arbitrary"),
        name="dn_conv_prep",
    )(proj3, proj3, state, w)


def _scan_rows(x, row, limit):
    d = 1
    while d < limit:
        x = x + jnp.where(row >= d, pltpu.roll(x, d, axis=0), 0.0)
        d *= 2
    return x


def _gates_kernel(x_ref, bias_ref, alog_ref, gates_ref, cum_ref, gcl_ref, carry_ref, *, tt, chunk):
    t = pl.program_id(1)

    @pl.when(t == 0)
    def _():
        carry_ref[...] = jnp.zeros_like(carry_ref)

    x = x_ref[0] + bias_ref[...]
    lane = lax.broadcasted_iota(jnp.int32, x.shape, 1)
    row = lax.broadcasted_iota(jnp.int32, x.shape, 0)
    lp = jnp.log1p(jnp.exp(-jnp.abs(x)))
    softplus = jnp.maximum(x, 0.0) + lp
    logsig = jnp.minimum(x, 0.0) - lp
    decay = -jnp.exp(alog_ref[...]) * softplus
    out = jnp.where(lane < H_DN, _sigmoid(x),
                    jnp.where(lane < 2 * H_DN, decay,
                              jnp.where(lane < 2 * H_DN + H_FOX, logsig, 0.0)))
    gates_ref[0] = out
    c = _scan_rows(out, row, tt) + carry_ref[...]
    cum_ref[0] = c
    carry_ref[...] = c[tt - 1:tt, :]
    gcl_ref[0] = _scan_rows(out, row % chunk, min(chunk, tt))


def _gates(proj3, bias, alog, *, tt):
    b, t, _ = proj3.shape
    spec = pl.BlockSpec((1, tt, LANE), lambda i, j: (i, j, 0))
    return pl.pallas_call(
        functools.partial(_gates_kernel, tt=tt, chunk=DN_CHUNK),
        out_shape=[jax.ShapeDtypeStruct((b, t, LANE), F32)] * 3,
        grid=(b, t // tt),
        in_specs=[
            pl.BlockSpec((1, tt, LANE), lambda i, j: (i, j, BLK_GATE)),
            pl.BlockSpec((1, LANE), lambda i, j: (0, 0)),
            pl.BlockSpec((1, LANE), lambda i, j: (0, 0)),
        ],
        out_specs=[spec, spec, spec],
        scratch_shapes=[pltpu.VMEM((1, LANE), F32)],
        compiler_params=_cparams("parallel", "arbitrary"),
        name="gate_act",
    )(proj3, bias, alog)


def _gdn_kernel(q_ref, k_ref, v_ref, z_ref, gates_ref, gcl_ref, gclt_ref, s0_ref, nw_ref, o_ref, s_ref, *, c, nck):
    @pl.when(pl.program_id(1) == 0)
    def _():
        s_ref[...] = s0_ref[...]

    ii = lax.broadcasted_iota(jnp.int32, (c, c), 0)
    jj = lax.broadcasted_iota(jnp.int32, (c, c), 1)
    lower = ii >= jj
    strict = ii > jj
    eye = jnp.where(ii == jj, 1.0, 0.0).astype(F32)
    n_sq = c.bit_length() - 2
    heads = range(H_DN)
    hsl = _head_slices(H_DN)
    pairs = [(ck, h) for ck in range(nck) for h in heads]
    rows = [slice(ck * c, (ck + 1) * c) for ck in range(nck)]
    q = {p: q_ref[0, rows[p[0]], hsl[p[1]]] for p in pairs}
    k = {p: k_ref[0, rows[p[0]], hsl[p[1]]] for p in pairs}
    v = {p: v_ref[0, rows[p[0]], hsl[p[1]]] for p in pairs}
    beta = {p: gates_ref[0, rows[p[0]], p[1]:p[1] + 1] for p in pairs}
    gc = {p: gcl_ref[0, rows[p[0]], H_DN + p[1]:H_DN + p[1] + 1] for p in pairs}
    gr = {p: gclt_ref[0, p[0], p[1]:p[1] + 1, :] for p in pairs}
    gl = {p: gr[p][:, c - 1:c] for p in pairs}
    gamma = {p: jnp.exp(jnp.where(lower, gc[p] - gr[p], -jnp.inf)) for p in pairs}
    kb = {p: k[p] * beta[p] for p in pairs}
    kbf = {p: k[p].astype(BF16) for p in pairs}
    kk = {p: _dot_nt(kb[p].astype(BF16), kbf[p]) for p in pairs}
    mm = {p: jnp.where(strict, -(kk[p] * gamma[p]), 0.0) for p in pairs}
    tinv = {p: eye + mm[p] for p in pairs}
    ps = {p: _split(mm[p]) for p in pairs}
    for _ in range(n_sq):
        ps = {p: _split(_dot3(ps[p], ps[p])) for p in pairs}
        tinv = {p: tinv[p] + _dot3(_split(tinv[p]), ps[p]) for p in pairs}
    egc = {p: jnp.exp(gc[p]) for p in pairs}
    rhs = {p: _split(jnp.concatenate([kb[p] * egc[p], v[p] * beta[p]], axis=1)) for p in pairs}
    wu = {p: _dot3(_split(tinv[p]), rhs[p]) for p in pairs}
    qk = {p: (_dot_nt(q[p].astype(BF16), kbf[p]) * gamma[p]).astype(BF16) for p in pairs}
    wb = {p: wu[p][:, :HEAD_DIM].astype(BF16) for p in pairs}
    qg = {p: (q[p] * egc[p]).astype(BF16) for p in pairs}
    k_dec = {p: (k[p] * jnp.exp(gl[p] - gc[p])).astype(BF16) for p in pairs}
    s = [s_ref[0, h] for h in heads]
    for ck in range(nck):
        sb = [x.astype(BF16) for x in s]
        v_new = [wu[ck, h][:, HEAD_DIM:] - _dot(wb[ck, h], sb[h]) for h in heads]
        vnb = [x.astype(BF16) for x in v_new]
        s = [s[h] * jnp.exp(gl[ck, h]) + _dot_tn(k_dec[ck, h], vnb[h]) for h in heads]
        for h in heads:
            o = _dot(qg[ck, h], sb[h]) + _dot(qk[ck, h], vnb[h])
            o_ref[0, rows[ck], hsl[h]] = (_rms(o, nw_ref[...]) * _silu(z_ref[0, rows[ck], hsl[h]])).astype(o_ref.dtype)
    for h in heads:
        s_ref[0, h] = s[h]


def _gdn(q, k, v, z, z_blk, gates, gcl, gclt, s0, nw, *, nck):
    b, t, _ = q.shape
    c = DN_CHUNK
    tc = nck * c
    blk = pl.BlockSpec((1, tc, DN_W), lambda i, j: (i, j, 0))
    gblk = pl.BlockSpec((1, tc, LANE), lambda i, j: (i, j, 0))
    sblk = pl.BlockSpec((1, H_DN, HEAD_DIM, HEAD_DIM), lambda i, j: (i, 0, 0, 0))
    return pl.pallas_call(
        functools.partial(_gdn_kernel, c=c, nck=nck),
        out_shape=[jax.ShapeDtypeStruct((b, t, DN_W), BF16), jax.ShapeDtypeStruct(s0.shape, F32)],
        grid=(b, t // tc),
        in_specs=[
            blk, blk, blk,
            pl.BlockSpec((1, tc, DN_W), lambda i, j: (i, j, z_blk)),
            gblk, gblk,
            pl.BlockSpec((1, nck, 8, c), lambda i, j: (i, j, 0, 0)),
            sblk,
            pl.BlockSpec((1, HEAD_DIM), lambda i, j: (0, 0)),
        ],
        out_specs=[blk, sblk],
        compiler_params=_cparams("parallel", "arbitrary"),
        name="gated_delta_rule",
    )(q, k, v, z, gates, gcl, gclt, s0, nw)


def _fox_kernel(q_ref, k_ref, v_ref, ct_ref, nw_ref, o_ref, qb_ref, m_ref, l_ref, acc_ref, *, tq):
    qi = pl.program_id(1)
    heads = range(H_FOX)
    hsl = _head_slices(H_FOX)
    row = lax.broadcasted_iota(jnp.int32, (tq, tq), 0)
    col = lax.broadcasted_iota(jnp.int32, (tq, tq), 1)
    ones = jnp.ones((tq, HEAD_DIM), BF16)
    qb_ref[...] = q_ref[0].astype(BF16)
    m_ref[...] = jnp.full_like(m_ref, -jnp.inf)
    l_ref[...] = jnp.zeros_like(l_ref)
    acc_ref[...] = jnp.zeros_like(acc_ref)

    def step(j, diagonal):
        ks = pl.ds(pl.multiple_of(j * tq, tq), tq)
        for g0 in range(0, H_FOX, FOX_HEAD_GROUP):
            grp = range(g0, g0 + FOX_HEAD_GROUP)
            s = {h: _dot_nt(qb_ref[:, hsl[h]], k_ref[0, ks, hsl[h]].astype(BF16)) * SCALE - ct_ref[0, j, h:h + 1, :]
                 for h in grp}
            if diagonal:
                s = {h: jnp.where(col <= row, s[h], -jnp.inf) for h in grp}
            m_old = {h: m_ref[h] for h in grp}
            m_new = {h: jnp.maximum(m_old[h], jnp.max(s[h], axis=-1, keepdims=True)) for h in grp}
            p = {h: jnp.exp(s[h] - m_new[h]) for h in grp}
            a = {h: jnp.exp(m_old[h] - m_new[h]) for h in grp}
            for h in grp:
                pb = p[h].astype(BF16)
                m_ref[h] = m_new[h]
                l_ref[h] = a[h] * l_ref[h] + _dot(pb, ones)
                acc_ref[h] = a[h] * acc_ref[h] + _dot(pb, v_ref[0, ks, hsl[h]].astype(BF16))

    def body(j, carry):
        step(j, False)
        return carry

    lax.fori_loop(0, qi, body, 0)
    step(qi, True)
    for h in heads:
        o_ref[0, :, hsl[h]] = _rms(acc_ref[h] / l_ref[h], nw_ref[...]).astype(o_ref.dtype)


def _fox_prompt(proj3, ct, nw, *, tq):
    b, t, _ = proj3.shape
    return pl.pallas_call(
        functools.partial(_fox_kernel, tq=tq),
        out_shape=jax.ShapeDtypeStruct((b, t, FOX_W), BF16),
        grid=(b, t // tq),
        in_specs=[
            pl.BlockSpec((1, tq, FOX_W), lambda i, j: (i, j, BLK_FQ // H_FOX)),
            pl.BlockSpec((1, t, FOX_W), lambda i, j: (i, 0, BLK_FK // H_FOX)),
            pl.BlockSpec((1, t, FOX_W), lambda i, j: (i, 0, BLK_FV // H_FOX)),
            pl.BlockSpec((1, t // tq, 8, tq), lambda i, j: (i, 0, 0, 0)),
            pl.BlockSpec((1, HEAD_DIM), lambda i, j: (0, 0)),
        ],
        out_specs=pl.BlockSpec((1, tq, FOX_W), lambda i, j: (i, j, 0)),
        scratch_shapes=[
            pltpu.VMEM((tq, FOX_W), BF16),
            pltpu.VMEM((H_FOX, tq, 1), F32),
            pltpu.VMEM((H_FOX, tq, HEAD_DIM), F32),
            pltpu.VMEM((H_FOX, tq, HEAD_DIM), F32),
        ],
        compiler_params=_cparams("parallel", "arbitrary"),
        name="fox_prompt",
    )(proj3, proj3, proj3, ct, nw)


def _suffix_incl(lg, tri):
    hi, lo = _split(lg)
    return _dot(hi, tri) + _dot(lo, tri)


def _sb_kernel(q_ref, k_ref, v_ref, nw_ref, o_ref, qb_ref, r_ref, acc_ref, *, tq):
    qi = pl.program_id(1)
    heads = range(H_SB)
    hsl = _head_slices(H_SB)
    row = lax.broadcasted_iota(jnp.int32, (tq, tq), 0)
    col = lax.broadcasted_iota(jnp.int32, (tq, tq), 1)
    tri = jnp.where(row >= col, 1.0, 0.0).astype(BF16)
    qb_ref[...] = q_ref[0].astype(BF16)
    r_ref[...] = jnp.zeros_like(r_ref)
    acc_ref[...] = jnp.zeros_like(acc_ref)

    def step(jr, diagonal):
        ks = pl.ds(pl.multiple_of((qi - jr) * tq, tq), tq)
        z = [_dot_nt(qb_ref[:, hsl[h]], k_ref[0, ks, hsl[h]].astype(BF16)) * SCALE for h in heads]
        lg = [_neg_softplus(x) for x in z]
        if diagonal:
            lg = [jnp.where(col < row, x, 0.0) for x in lg]
        incl = [_suffix_incl(x, tri) for x in lg]
        a = [jnp.exp(z[h] + incl[h] + r_ref[h]) for h in heads]
        if diagonal:
            a = [jnp.where(col < row, x, 0.0) for x in a]
        for h in heads:
            acc_ref[h] += _dot(a[h].astype(BF16), v_ref[0, ks, hsl[h]].astype(BF16))
            r_ref[h] += incl[h][:, 0:1]

    def body(jr, carry):
        step(jr, False)
        return carry

    step(0, True)
    lax.fori_loop(1, qi + 1, body, 0)
    for h in heads:
        o_ref[0, :, hsl[h]] = _rms(acc_ref[h], nw_ref[...]).astype(o_ref.dtype)


def _sb_prompt(proj3, nw, *, tq):
    b, t, _ = proj3.shape
    return pl.pallas_call(
        functools.partial(_sb_kernel, tq=tq),
        out_shape=jax.ShapeDtypeStruct((b, t, SB_W), BF16),
        grid=(b, t // tq),
        in_specs=[
            pl.BlockSpec((1, tq, SB_W), lambda i, j: (i, j, BLK_SQ // H_SB)),
            pl.BlockSpec((1, t, SB_W), lambda i, j: (i, 0, BLK_SK // H_SB)),
            pl.BlockSpec((1, t, SB_W), lambda i, j: (i, 0, BLK_SV // H_SB)),
            pl.BlockSpec((1, HEAD_DIM), lambda i, j: (0, 0)),
        ],
        out_specs=pl.BlockSpec((1, tq, SB_W), lambda i, j: (i, j, 0)),
        scratch_shapes=[
            pltpu.VMEM((tq, SB_W), BF16),
            pltpu.VMEM((H_SB, tq, 1), F32),
            pltpu.VMEM((H_SB, tq, HEAD_DIM), F32),
        ],
        compiler_params=_cparams("parallel", "arbitrary"),
        name="sb_prompt",
    )(proj3, proj3, proj3, nw)


def _page_rows(page_ref, head_major=False):
    x = page_ref[...] if head_major else jnp.swapaxes(page_ref[...], 0, 1)
    return jnp.concatenate([x[h] for h in range(x.shape[0])], axis=1).astype(BF16)


def _block_diag_q(q_ref, nh, t):
    zero = jnp.zeros((t, HEAD_DIM), F32)
    rows = [jnp.concatenate([q_ref[0, :, h * HEAD_DIM:(h + 1) * HEAD_DIM] if j == h else zero
                             for j in range(nh)], axis=1) for h in range(nh)]
    return jnp.concatenate(rows, axis=0).astype(BF16)


def _rows_per_head(x, nh, t):
    return jnp.concatenate([jnp.broadcast_to(x[h:h + 1, :], (t, x.shape[1])) for h in range(nh)], axis=0)


def _scores(qblk, keys):
    return _dot_nt(qblk, keys) * SCALE


def _weighted_values(w, values, nh, t):
    full = _dot(w.astype(BF16), values)
    return jnp.concatenate([full[h * t:(h + 1) * t, h * HEAD_DIM:(h + 1) * HEAD_DIM] for h in range(nh)], axis=0)


def _fox_dec_kernel(pt_ref, q_ref, kn_ref, vn_ref, cn_ref, nw_ref, *rest, g, t):
    kp, vp, lp = rest[:g], rest[g:2 * g], rest[2 * g:3 * g]
    o_ref = rest[3 * g]
    m_ref, l_ref, acc_ref, off_ref, kpad_ref, vpad_ref = rest[3 * g + 1:]
    nh = H_FOX
    hsl = _head_slices(nh)
    p = pl.program_id(1)

    @pl.when(p == 0)
    def _():
        m_ref[...] = jnp.full_like(m_ref, -jnp.inf)
        l_ref[...] = jnp.zeros_like(l_ref)
        acc_ref[...] = jnp.zeros_like(acc_ref)
        off_ref[...] = jnp.zeros_like(off_ref)

    qblk = _block_diag_q(q_ref, nh, t)
    ii = lax.broadcasted_iota(jnp.int32, (PAGE, PAGE), 0)
    jj = lax.broadcasted_iota(jnp.int32, (PAGE, PAGE), 1)
    upper = jnp.where(ii <= jj, 1.0, 0.0).astype(F32)

    def update(s, weighted_values):
        m = m_ref[...]
        m_new = jnp.maximum(m, jnp.max(s, axis=-1, keepdims=True))
        pr = jnp.exp(s - m_new)
        a = jnp.exp(m - m_new)
        l_ref[...] = a * l_ref[...] + jnp.sum(pr, axis=-1, keepdims=True)
        acc_ref[...] = a * acc_ref[...] + weighted_values(pr)
        m_ref[...] = m_new

    cw_all = _dot_hi(jnp.concatenate([lp[i][...] for i in range(g)], axis=0), upper)
    off = off_ref[...]
    bias = []
    for i in range(g):
        cw = cw_all[8 * i:8 * (i + 1)]
        bias.append(_rows_per_head(off + cw, nh, t))
        off = off + cw[:, PAGE - 1:PAGE]
    off_ref[...] = off
    keys = jnp.concatenate([_page_rows(kp[i], True) for i in range(g)], axis=0)
    values = jnp.concatenate([_page_rows(vp[i], True) for i in range(g)], axis=0)
    update(_scores(qblk, keys) - jnp.concatenate(bias, axis=1), lambda pr: _weighted_values(pr, values, nh, t))

    @pl.when(p == pl.num_programs(1) - 1)
    def _():
        kpad_ref[...] = jnp.zeros_like(kpad_ref)
        vpad_ref[...] = jnp.zeros_like(vpad_ref)
        kpad_ref[0:t, :] = kn_ref[0]
        vpad_ref[0:t, :] = vn_ref[0]
        c = off_ref[...] + cn_ref[0]
        s = _scores(qblk, kpad_ref[...].astype(BF16)) - _rows_per_head(c, nh, t)
        r = lax.broadcasted_iota(jnp.int32, s.shape, 0)
        cc = lax.broadcasted_iota(jnp.int32, s.shape, 1)
        s = jnp.where(cc <= r % t, s, -jnp.inf)
        update(s, lambda pr: _weighted_values(pr, vpad_ref[...].astype(BF16), nh, t))
        o = acc_ref[...] / l_ref[...]
        for h in range(nh):
            o_ref[0, :, hsl[h]] = _rms(o[h * t:(h + 1) * t], nw_ref[...]).astype(o_ref.dtype)


def _fox_dec(page_table, proj3, cnt, nw, k_pool, v_pool, lf_pool, layer, *, g):
    b, t, _ = proj3.shape
    npg = page_table.shape[1]
    nh = H_FOX

    def page_spec(shape, i):
        return pl.BlockSpec((None, None) + shape, lambda bi, p, pt, i=i: (layer, pt[bi, p * g + i], 0, 0, 0)[:2 + len(shape)])

    in_specs = [
        pl.BlockSpec((1, t, FOX_W), lambda bi, p, pt: (bi, 0, BLK_FQ // nh)),
        pl.BlockSpec((1, t, FOX_W), lambda bi, p, pt: (bi, 0, BLK_FK // nh)),
        pl.BlockSpec((1, t, FOX_W), lambda bi, p, pt: (bi, 0, BLK_FV // nh)),
        pl.BlockSpec((1, 8, PAGE), lambda bi, p, pt: (bi, 0, 0)),
        pl.BlockSpec((1, HEAD_DIM), lambda bi, p, pt: (0, 0)),
    ]
    in_specs += [page_spec((nh, PAGE, HEAD_DIM), i) for i in range(g)]
    in_specs += [page_spec((nh, PAGE, HEAD_DIM), i) for i in range(g)]
    in_specs += [page_spec((8, PAGE), i) for i in range(g)]
    grid_spec = pltpu.PrefetchScalarGridSpec(
        num_scalar_prefetch=1,
        grid=(b, npg // g),
        in_specs=in_specs,
        out_specs=pl.BlockSpec((1, t, FOX_W), lambda bi, p, pt: (bi, 0, 0)),
        scratch_shapes=[
            pltpu.VMEM((nh * t, 1), F32),
            pltpu.VMEM((nh * t, 1), F32),
            pltpu.VMEM((nh * t, HEAD_DIM), F32),
            pltpu.VMEM((8, 1), F32),
            pltpu.VMEM((PAGE, FOX_W), F32),
            pltpu.VMEM((PAGE, FOX_W), F32),
        ],
    )
    return pl.pallas_call(
        functools.partial(_fox_dec_kernel, g=g, t=t),
        out_shape=jax.ShapeDtypeStruct((b, t, FOX_W), BF16),
        grid_spec=grid_spec,
        compiler_params=_cparams("parallel", "arbitrary"),
        name="fox_paged",
    )(page_table, proj3, proj3, proj3, cnt, nw, *([k_pool] * g), *([v_pool] * g), *([lf_pool] * g))


def _sb_dec_kernel(pt_ref, q_ref, kn_ref, vn_ref, nw_ref, *rest, g, t):
    kp, vp = rest[:g], rest[g:2 * g]
    o_ref = rest[2 * g]
    rsum_ref, acc_ref, kpad_ref, vpad_ref = rest[2 * g + 1:]
    nh = H_SB
    hsl = _head_slices(nh)
    p = pl.program_id(1)
    ii = lax.broadcasted_iota(jnp.int32, (PAGE, PAGE), 0)
    jj = lax.broadcasted_iota(jnp.int32, (PAGE, PAGE), 1)
    tri = jnp.where(ii >= jj, 1.0, 0.0).astype(BF16)
    qblk = _block_diag_q(q_ref, nh, t)

    @pl.when(p == 0)
    def _():
        kpad_ref[...] = jnp.zeros_like(kpad_ref)
        vpad_ref[...] = jnp.zeros_like(vpad_ref)
        kpad_ref[0:t, :] = kn_ref[0]
        vpad_ref[0:t, :] = vn_ref[0]
        z = _scores(qblk, kpad_ref[...].astype(BF16))
        r = lax.broadcasted_iota(jnp.int32, z.shape, 0)
        cc = lax.broadcasted_iota(jnp.int32, z.shape, 1)
        valid = cc < r % t
        incl = _suffix_incl(jnp.where(valid, _neg_softplus(z), 0.0), tri)
        a = jnp.where(valid, jnp.exp(z + incl), 0.0)
        rsum_ref[...] = incl[:, 0:1]
        acc_ref[...] = _weighted_values(a, vpad_ref[...].astype(BF16), nh, t)

    keys = jnp.concatenate([_page_rows(kp[i]) for i in range(g)], axis=0)
    z = _scores(qblk, keys)
    lg = _neg_softplus(z)
    incl_rows = _suffix_incl(jnp.concatenate([lg[:, i * PAGE:(i + 1) * PAGE] for i in range(g)], axis=0), tri)
    rsum = rsum_ref[...]
    weights = []
    for i in range(g):
        incl = incl_rows[i * nh * t:(i + 1) * nh * t]
        weights.append(jnp.exp(z[:, i * PAGE:(i + 1) * PAGE] + incl + rsum))
        rsum = rsum + incl[:, 0:1]
    rsum_ref[...] = rsum
    values = jnp.concatenate([_page_rows(vp[i]) for i in range(g)], axis=0)
    acc_ref[...] += _weighted_values(jnp.concatenate(weights, axis=1), values, nh, t)

    @pl.when(p == pl.num_programs(1) - 1)
    def _():
        acc = acc_ref[...]
        for h in range(nh):
            o_ref[0, :, hsl[h]] = _rms(acc[h * t:(h + 1) * t], nw_ref[...]).astype(o_ref.dtype)


def _sb_dec(page_table, proj3, nw, k_pool, v_pool, layer, *, g):
    b, t, _ = proj3.shape
    npg = page_table.shape[1]
    nh = H_SB

    def page_spec(i):
        return pl.BlockSpec((None, None, PAGE, nh, HEAD_DIM),
                            lambda bi, p, pt, i=i: (layer, pt[bi, npg - 1 - (p * g + i)], 0, 0, 0))

    in_specs = [
        pl.BlockSpec((1, t, SB_W), lambda bi, p, pt: (bi, 0, BLK_SQ // nh)),
        pl.BlockSpec((1, t, SB_W), lambda bi, p, pt: (bi, 0, BLK_SK // nh)),
        pl.BlockSpec((1, t, SB_W), lambda bi, p, pt: (bi, 0, BLK_SV // nh)),
        pl.BlockSpec((1, HEAD_DIM), lambda bi, p, pt: (0, 0)),
    ]
    in_specs += [page_spec(i) for i in range(g)] * 2
    grid_spec = pltpu.PrefetchScalarGridSpec(
        num_scalar_prefetch=1,
        grid=(b, npg // g),
        in_specs=in_specs,
        out_specs=pl.BlockSpec((1, t, SB_W), lambda bi, p, pt: (bi, 0, 0)),
        scratch_shapes=[
            pltpu.VMEM((nh * t, 1), F32),
            pltpu.VMEM((nh * t, HEAD_DIM), F32),
            pltpu.VMEM((PAGE, SB_W), F32),
            pltpu.VMEM((PAGE, SB_W), F32),
        ],
    )
    return pl.pallas_call(
        functools.partial(_sb_dec_kernel, g=g, t=t),
        out_shape=jax.ShapeDtypeStruct((b, t, SB_W), BF16),
        grid_spec=grid_spec,
        compiler_params=_cparams("parallel", "arbitrary"),
        name="sb_paged",
    )(page_table, proj3, proj3, proj3, nw, *([k_pool] * g), *([v_pool] * g))


def _head_major_kernel(*refs, nh):
    x_ref, o_ref = refs[0], refs[-1]
    for h in range(nh):
        o_ref[0, h] = x_ref[0, :, h * HEAD_DIM:(h + 1) * HEAD_DIM]


def _head_major_rows(proj3, col_blk, nh, layer, depth, prev, *, tt):
    b, t, _ = proj3.shape
    w = nh * HEAD_DIM
    in_specs = [pl.BlockSpec((1, tt, w), lambda i, j: (i, j, col_blk * LANE // w))]
    args = [proj3]
    aliases = {}
    if prev is not None:
        in_specs.append(pl.BlockSpec(memory_space=pl.ANY))
        args.append(prev)
        aliases = {1: 0}
    return pl.pallas_call(
        functools.partial(_head_major_kernel, nh=nh),
        out_shape=jax.ShapeDtypeStruct((depth, b, nh, t, HEAD_DIM), F32),
        grid=(b, t // tt),
        in_specs=in_specs,
        out_specs=pl.BlockSpec((None, 1, nh, tt, HEAD_DIM), lambda i, j: (layer, i, 0, j, 0)),
        input_output_aliases=aliases,
        compiler_params=_cparams("parallel", "parallel"),
        name="head_major_rows",
    )(*args)


def _out_proj_kernel(x_ref, a_ref, b_ref, c_ref, w_ref, g_ref, o_ref):
    acc = _dot(a_ref[...], w_ref[0:DN_W, :])
    acc += _dot(b_ref[...], w_ref[DN_W:DN_W + FOX_W, :])
    acc += _dot(c_ref[...], w_ref[DN_W + FOX_W:, :])
    o_ref[...] = x_ref[...] + _rms(acc, g_ref[...])


def _out_proj(x, oa, ob, oc, w, g, *, tm):
    m, d = x.shape
    return pl.pallas_call(
        _out_proj_kernel,
        out_shape=jax.ShapeDtypeStruct((m, d), F32),
        grid=(m // tm,),
        in_specs=[
            pl.BlockSpec((tm, d), lambda i: (i, 0)),
            pl.BlockSpec((tm, DN_W), lambda i: (i, 0)),
            pl.BlockSpec((tm, FOX_W), lambda i: (i, 0)),
            pl.BlockSpec((tm, SB_W), lambda i: (i, 0)),
            pl.BlockSpec(w.shape, lambda i: (0, 0)),
            pl.BlockSpec((1, d), lambda i: (0, 0)),
        ],
        out_specs=pl.BlockSpec((tm, d), lambda i: (i, 0)),
        compiler_params=_cparams("parallel"),
        name="out_proj",
    )(x, oa, ob, oc, w, g)


def _ffn_tail(c, x_ref, gpost_ref, acc_ref, o_ref):
    @pl.when(c == pl.num_programs(1) - 1)
    def _():
        o_ref[...] = x_ref[...] + _rms(acc_ref[...], gpost_ref[...])


def _ffn_prompt_kernel(x_ref, halo_ref, gpre_ref, wg_ref, wu_ref, cg_ref, cu_ref, wd_ref, gpost_ref,
                       o_ref, st_ref, hn_ref, acc_ref, ug_ref, uu_ref, *, tm, tiles_per_seq):
    i = pl.program_id(0)
    c = pl.program_id(1)

    @pl.when(c == 0)
    def _():
        hn_ref[pl.ds(8, tm), :] = _rms(x_ref[...], gpre_ref[...]).astype(BF16)
        keep = jnp.where(i % tiles_per_seq == 0, 0.0, 1.0)
        hn_ref[0:8, :] = (_rms(halo_ref[...], gpre_ref[...]) * keep).astype(BF16)
        acc_ref[...] = jnp.zeros_like(acc_ref)

    hn = hn_ref[...]
    ug_ref[...] = _dot(hn, wg_ref[...])
    uu_ref[...] = _dot(hn, wu_ref[...])

    def conv(u_ref, w_ref):
        y = u_ref[pl.ds(8 - (FFN_CONV - 1), tm), :] * w_ref[0:1, :]
        for k in range(1, FFN_CONV):
            y = y + u_ref[pl.ds(8 - (FFN_CONV - 1) + k, tm), :] * w_ref[k:k + 1, :]
        return y

    act = _silu(conv(ug_ref, cg_ref)) * conv(uu_ref, cu_ref)
    acc_ref[...] += _dot(act.astype(BF16), wd_ref[...])
    st_ref[0, 0] = ug_ref[pl.ds(tm + 8 - (FFN_CONV - 1), FFN_CONV - 1), :]
    st_ref[0, 1] = uu_ref[pl.ds(tm + 8 - (FFN_CONV - 1), FFN_CONV - 1), :]
    _ffn_tail(c, x_ref, gpost_ref, acc_ref, o_ref)


def _ffn_prompt(x, seq_len, gpre, w_up, conv_w, w_down, gpost, *, tm, tc):
    m, d = x.shape
    dff = w_down.shape[0]
    nc = dff // tc
    tps = seq_len // tm
    hb = tm // 8
    return pl.pallas_call(
        functools.partial(_ffn_prompt_kernel, tm=tm, tiles_per_seq=tps),
        out_shape=[jax.ShapeDtypeStruct((m, d), F32),
                   jax.ShapeDtypeStruct((m // tm, 2, FFN_CONV - 1, dff), F32)],
        grid=(m // tm, nc),
        in_specs=[
            pl.BlockSpec((tm, d), lambda i, c: (i, 0)),
            pl.BlockSpec((8, d), lambda i, c: (jnp.maximum(i * hb - 1, 0), 0)),
            pl.BlockSpec((1, d), lambda i, c: (0, 0)),
            pl.BlockSpec((d, tc), lambda i, c: (0, c)),
            pl.BlockSpec((d, tc), lambda i, c: (0, nc + c)),
            pl.BlockSpec((FFN_CONV, tc), lambda i, c: (0, c)),
            pl.BlockSpec((FFN_CONV, tc), lambda i, c: (0, nc + c)),
            pl.BlockSpec((tc, d), lambda i, c: (c, 0)),
            pl.BlockSpec((1, d), lambda i, c: (0, 0)),
        ],
        out_specs=[
            pl.BlockSpec((tm, d), lambda i, c: (i, 0)),
            pl.BlockSpec((1, 2, FFN_CONV - 1, tc), lambda i, c: (i, 0, 0, c)),
        ],
        scratch_shapes=[
            pltpu.VMEM((tm + 8, d), BF16),
            pltpu.VMEM((tm, d), F32),
            pltpu.VMEM((tm + 8, tc), F32),
            pltpu.VMEM((tm + 8, tc), F32),
        ],
        compiler_params=_cparams("parallel", "arbitrary"),
        name="conv_ffn_prompt",
    )(x, x, gpre, w_up, w_up, conv_w, conv_w, w_down, gpost)


def _ffn_dec_kernel(x_ref, gpre_ref, wg_ref, wu_ref, cg_ref, cu_ref, wd_ref, gpost_ref,
                    s0g_ref, s1g_ref, s0u_ref, s1u_ref,
                    o_ref, ugo_ref, uuo_ref, hn_ref, acc_ref, ug_ref, uu_ref, *, tm, t):
    c = pl.program_id(1)

    @pl.when(c == 0)
    def _():
        hn_ref[...] = _rms(x_ref[...], gpre_ref[...]).astype(BF16)
        acc_ref[...] = jnp.zeros_like(acc_ref)
        ug_ref[0:8, :] = jnp.zeros((8, ug_ref.shape[1]), F32)
        uu_ref[0:8, :] = jnp.zeros((8, uu_ref.shape[1]), F32)

    hn = hn_ref[...]
    ug = _dot(hn, wg_ref[...])
    uu = _dot(hn, wu_ref[...])
    ugo_ref[...] = ug
    uuo_ref[...] = uu
    ug_ref[pl.ds(8, tm), :] = ug
    uu_ref[pl.ds(8, tm), :] = uu
    pos = lax.broadcasted_iota(jnp.int32, ug.shape, 0) % t

    def conv(u, u_ref, s0_ref, s1_ref, w_ref):
        um1 = jnp.where(pos >= 1, u_ref[pl.ds(7, tm), :], s1_ref[...])
        um2 = jnp.where(pos >= 2, u_ref[pl.ds(6, tm), :], jnp.where(pos == 0, s0_ref[...], s1_ref[...]))
        return um2 * w_ref[0:1, :] + um1 * w_ref[1:2, :] + u * w_ref[2:3, :]

    act = _silu(conv(ug, ug_ref, s0g_ref, s1g_ref, cg_ref)) * conv(uu, uu_ref, s0u_ref, s1u_ref, cu_ref)
    acc_ref[...] += _dot(act.astype(BF16), wd_ref[...])
    _ffn_tail(c, x_ref, gpost_ref, acc_ref, o_ref)


def _ffn_dec(x, t, gpre, w_up, conv_w, w_down, gpost, s0e, s1e, *, tc):
    m, d = x.shape
    dff = w_down.shape[0]
    nc = dff // tc
    tm = m
    st = lambda off: pl.BlockSpec((tm, tc), lambda i, c, off=off: (0, off * nc + c))
    return pl.pallas_call(
        functools.partial(_ffn_dec_kernel, tm=tm, t=t),
        out_shape=[jax.ShapeDtypeStruct((m, d), F32),
                   jax.ShapeDtypeStruct((m, dff), F32), jax.ShapeDtypeStruct((m, dff), F32)],
        grid=(1, nc),
        in_specs=[
            pl.BlockSpec((tm, d), lambda i, c: (0, 0)),
            pl.BlockSpec((1, d), lambda i, c: (0, 0)),
            pl.BlockSpec((d, tc), lambda i, c: (0, c)),
            pl.BlockSpec((d, tc), lambda i, c: (0, nc + c)),
            pl.BlockSpec((FFN_CONV, tc), lambda i, c: (0, c)),
            pl.BlockSpec((FFN_CONV, tc), lambda i, c: (0, nc + c)),
            pl.BlockSpec((tc, d), lambda i, c: (c, 0)),
            pl.BlockSpec((1, d), lambda i, c: (0, 0)),
            st(0), st(0), st(1), st(1),
        ],
        out_specs=[
            pl.BlockSpec((tm, d), lambda i, c: (0, 0)),
            pl.BlockSpec((tm, tc), lambda i, c: (0, c)),
            pl.BlockSpec((tm, tc), lambda i, c: (0, c)),
        ],
        scratch_shapes=[
            pltpu.VMEM((tm, d), BF16),
            pltpu.VMEM((tm, d), F32),
            pltpu.VMEM((tm + 8, tc), F32),
            pltpu.VMEM((tm + 8, tc), F32),
        ],
        compiler_params=_cparams("arbitrary", "arbitrary"),
        name="conv_ffn_sample",
    )(x, gpre, w_up, w_up, conv_w, conv_w, w_down, gpost, s0e, s1e, s0e, s1e)


def _pack_w_in(w_in):
    d = w_in.shape[0]
    wt = w_in.T
    sizes = (3 * DN_W, H_DN, H_DN, DN_W, FOX_W, FOX_W, FOX_W, H_FOX, SB_W, SB_W, SB_W)
    parts, start = [], 0
    for s in sizes:
        parts.append(wt[start:start + s])
        start += s
    qkv, dn_b, dn_a, dn_z, fq, fk, fv, ff, sq, sk, sv = parts
    gate = jnp.concatenate([dn_b, dn_a, ff, jnp.zeros((2 * LANE - 2 * H_DN - H_FOX, d), w_in.dtype)], axis=0)
    return jnp.concatenate([qkv, dn_z, fq, fk, fv, gate, sq, sk, sv], axis=0).astype(BF16)


def _lane_row(pairs):
    row = jnp.zeros((LANE,), F32)
    for off, v in pairs:
        row = row.at[off:off + v.shape[0]].set(v.astype(F32))
    return row[None, :]


def _mixers_common(x3, lw, dn_state, tm, tt):
    b, t, d = x3.shape
    proj = _norm_matmul(x3.reshape(b * t, d), lw["norm_pre_mix"], lw["w_in"], tm=tm, tn=1024)
    proj3 = proj.reshape(b, t, PACK_W)
    q, k, v, dn_buf_new = _dn_pre(proj3, dn_state, lw["dn_conv"], tt=tt)
    gates, cum, gcl = _gates(proj3, lw["gate_bias"], lw["alog_row"], tt=tt)
    return proj3, q, k, v, dn_buf_new, gates, cum, gcl


def _gclt(gcl, c):
    b, t, _ = gcl.shape
    g = gcl[:, :, H_DN:2 * H_DN].reshape(b, t // c, c, H_DN)
    g = jnp.swapaxes(g, 2, 3)
    return jnp.pad(g, ((0, 0), (0, 0), (0, 8 - H_DN), (0, 0)))


def _new_rows(proj3, gates):
    b, t, _ = proj3.shape

    def cols(blk, w, nh):
        return proj3[:, :, blk * LANE:blk * LANE + w].reshape(b, t, nh, HEAD_DIM)

    return (cols(BLK_FK, FOX_W, H_FOX), cols(BLK_FV, FOX_W, H_FOX), gates[:, :, 2 * H_DN:2 * H_DN + H_FOX],
            cols(BLK_SK, SB_W, H_SB), cols(BLK_SV, SB_W, H_SB))


def _layer_prompt(x3, lw, layer, depth, fox_bufs):
    b, t, d = x3.shape
    tq = 256
    tm_ffn = 512
    zeros_dn = jnp.zeros((b, DN_CONV - 1, 3 * DN_W), F32)
    proj3, q, k, v, dn_buf_new, gates, cum, gcl = _mixers_common(x3, lw, zeros_dn, tm=1024, tt=256)
    s0 = jnp.zeros((b, H_DN, HEAD_DIM, HEAD_DIM), F32)
    o_a, s_new = _gdn(q, k, v, proj3, BLK_Z // H_DN, gates, gcl, _gclt(gcl, DN_CHUNK), s0, lw["dn_norm"],
                      nck=GDN_CHUNKS_PER_STEP)
    ct = cum[:, :, 2 * H_DN:2 * H_DN + H_FOX].reshape(b, t // tq, tq, H_FOX)
    ct = jnp.pad(jnp.swapaxes(ct, 2, 3), ((0, 0), (0, 0), (0, 8 - H_FOX), (0, 0)))
    o_b = _fox_prompt(proj3, ct, lw["fox_norm"], tq=tq)
    o_c = _sb_prompt(proj3, lw["sb_norm"], tq=tq)
    m = b * t
    x1 = _out_proj(x3.reshape(m, d), o_a.reshape(m, DN_W), o_b.reshape(m, FOX_W), o_c.reshape(m, SB_W),
                   lw["w_out"], lw["norm_post_mix"], tm=512)
    y, st = _ffn_prompt(x1, t, lw["norm_pre_ffn"], lw["ffn_w_up"], lw["ffn_conv"], lw["ffn_w_down"],
                        lw["norm_post_ffn"], tm=tm_ffn, tc=512)
    tps = t // tm_ffn
    ffn_buf_new = jnp.swapaxes(st[tps - 1::tps], 1, 2).reshape(b, FFN_CONV - 1, -1)
    _, _, logf, sk, sv = _new_rows(proj3, gates)
    fox_bufs = (_head_major_rows(proj3, BLK_FK, H_FOX, layer, depth, fox_bufs[0], tt=512),
                _head_major_rows(proj3, BLK_FV, H_FOX, layer, depth, fox_bufs[1], tt=512))
    return y.reshape(b, t, d), (s_new, dn_buf_new, logf, sk, sv, ffn_buf_new), fox_bufs


def _layer_sample(x3, lw, dn_s0, dn_buf, ffn_buf, pools, page_table, layer):
    b, t, d = x3.shape
    c = DN_CHUNK
    fox_k, fox_v, fox_lf, sb_k, sb_v = pools
    proj3, q, k, v, dn_buf_new, gates, cum, gcl = _mixers_common(x3, lw, dn_buf, tm=b * t, tt=t)
    padt = ((0, 0), (0, c - t), (0, 0))
    z = proj3[:, :, BLK_Z * LANE:BLK_Z * LANE + DN_W]
    gcl_pad = jnp.pad(gcl, padt, mode="edge")
    o_a, s_new = _gdn(jnp.pad(q, padt), jnp.pad(k, padt), jnp.pad(v, padt), jnp.pad(z, padt), 0,
                      jnp.pad(gates, padt), gcl_pad, _gclt(gcl_pad, c), dn_s0, lw["dn_norm"], nck=1)
    o_a = o_a[:, :t]
    cnt = jnp.swapaxes(cum[:, :, 2 * H_DN:2 * H_DN + H_FOX], 1, 2)
    cnt = jnp.pad(cnt, ((0, 0), (0, 8 - H_FOX), (0, PAGE - t)))
    o_b = _fox_dec(page_table, proj3, cnt, lw["fox_norm"], fox_k, fox_v, fox_lf, layer, g=PAGES_PER_STEP)
    o_c = _sb_dec(page_table, proj3, lw["sb_norm"], sb_k, sb_v, layer, g=PAGES_PER_STEP)
    m = b * t
    x1 = _out_proj(x3.reshape(m, d), o_a.reshape(m, DN_W), o_b.reshape(m, FOX_W), o_c.reshape(m, SB_W),
                   lw["w_out"], lw["norm_post_mix"], tm=m)
    s0e = jnp.repeat(ffn_buf[:, 0, :], t, axis=0)
    s1e = jnp.repeat(ffn_buf[:, 1, :], t, axis=0)
    y, ug, uu = _ffn_dec(x1, t, lw["norm_pre_ffn"], lw["ffn_w_up"], lw["ffn_conv"], lw["ffn_w_down"],
                         lw["norm_post_ffn"], s0e, s1e, tc=512)
    u = jnp.concatenate([ug, uu], axis=1).reshape(b, t, -1)
    ffn_buf_new = u[:, t - (FFN_CONV - 1):]
    fk, fv, logf, sk, sv = _new_rows(proj3, gates)
    return y.reshape(b, t, d), (s_new, dn_buf_new, fk, fv, logf, sk, sv, ffn_buf_new)


def kernel(x_prompt, x_sample, state_dn_S, state_dn_conv, cache_fox_k, cache_fox_v, cache_fox_logf, cache_sb_k, cache_sb_v, state_ffn_conv, page_table, norm_pre_mix, w_in, dn_conv, dn_A_log, dn_dt_bias, dn_norm, fox_b_f, fox_norm, sb_norm, w_out, norm_post_mix, norm_pre_ffn, ffn_w_up, ffn_conv, ffn_w_down, norm_post_ffn):
    depth = w_in.shape[0]
    yp, ys = x_prompt, x_sample
    prompt_new, sample_new = [], []
    fox_bufs = (None, None)
    lf_t = jnp.pad(jnp.swapaxes(cache_fox_logf, 2, 3), ((0, 0), (0, 0), (0, 8 - H_FOX), (0, 0)))
    pools = (jnp.swapaxes(cache_fox_k, 2, 3), jnp.swapaxes(cache_fox_v, 2, 3), lf_t, cache_sb_k, cache_sb_v)
    for l in range(depth):
        lw = {
            "norm_pre_mix": norm_pre_mix[l][None, :],
            "w_in": _pack_w_in(w_in[l]),
            "dn_conv": dn_conv[l],
            "gate_bias": _lane_row([(H_DN, dn_dt_bias[l]), (2 * H_DN, fox_b_f[l])]),
            "alog_row": _lane_row([(H_DN, dn_A_log[l])]),
            "dn_norm": dn_norm[l][None, :],
            "fox_norm": fox_norm[l][None, :],
            "sb_norm": sb_norm[l][None, :],
            "w_out": w_out[l].astype(BF16),
            "norm_post_mix": norm_post_mix[l][None, :],
            "norm_pre_ffn": norm_pre_ffn[l][None, :],
            "ffn_w_up": ffn_w_up[l].astype(BF16),
            "ffn_conv": ffn_conv[l],
            "ffn_w_down": ffn_w_down[l].astype(BF16),
            "norm_post_ffn": norm_post_ffn[l][None, :],
        }
        yp, st_p, fox_bufs = _layer_prompt(yp, lw, l, depth, fox_bufs)
        prompt_new.append(st_p)
        ys, st_s = _layer_sample(ys, lw, state_dn_S[l], state_dn_conv[l], state_ffn_conv[l], pools, page_table, l)
        sample_new.append(st_s)
    dn_s_p, dn_conv_p, logf_p, sk_p, sv_p, ffn_p = [jnp.stack(a) for a in zip(*prompt_new)]
    fk_p, fv_p = [jnp.swapaxes(x, 2, 3) for x in fox_bufs]
    s_out = [jnp.stack(a) for a in zip(*sample_new)]
    return (yp, ys, dn_s_p, dn_conv_p, fk_p, fv_p, logf_p, sk_p, sv_p, ffn_p, *s_out)
```

```python
import functools

import jax
import jax.numpy as jnp
from jax import lax
from jax.experimental import pallas as pl
from jax.experimental.pallas import tpu as pltpu

F32 = jnp.float32
BF16 = jnp.bfloat16

HEAD_DIM = 128
H_DN = 6
H_FOX = 6
H_SB = 4
DN_W = H_DN * HEAD_DIM
FOX_W = H_FOX * HEAD_DIM
SB_W = H_SB * HEAD_DIM
DN_CONV = 4
FFN_CONV = 3
DN_CHUNK = 64
PAGE = 128
EPS = 1e-6
SCALE = HEAD_DIM ** -0.5
LANE = 128

BLK_QKV = 0
BLK_Z = 18
BLK_FQ = 24
BLK_FK = 30
BLK_FV = 36
BLK_GATE = 42
BLK_SQ = 44
BLK_SK = 48
BLK_SV = 52
N_BLK = 56
PACK_W = N_BLK * LANE

VMEM_LIMIT = 56 * 1024 * 1024
FOX_HEAD_GROUP = 6
GDN_CHUNKS_PER_STEP = 4
PAGES_PER_STEP = 16


def _cparams(*sem):
    return pltpu.CompilerParams(dimension_semantics=sem, vmem_limit_bytes=VMEM_LIMIT)


def _dot(a, b):
    return jnp.dot(a, b, preferred_element_type=F32)


def _dot_nt(a, b):
    return lax.dot_general(a, b, (((1,), (1,)), ((), ())), preferred_element_type=F32)


def _dot_tn(a, b):
    return lax.dot_general(a, b, (((0,), (0,)), ((), ())), preferred_element_type=F32)


def _dot_hi(a, b):
    return jnp.dot(a, b, preferred_element_type=F32, precision=lax.Precision.HIGHEST)


def _split(x):
    hi = x.astype(BF16)
    return hi, (x - hi.astype(F32)).astype(BF16)


def _dot3(a, b):
    return _dot(a[0], b[0]) + (_dot(a[0], b[1]) + _dot(a[1], b[0]))


def _sigmoid(x):
    return 1.0 / (1.0 + jnp.exp(-x))


def _silu(x):
    return x * _sigmoid(x)


def _neg_softplus(z):
    return -(jnp.maximum(z, 0.0) + jnp.log(1.0 + jnp.exp(-jnp.abs(z))))


def _rms(x, w):
    return x * lax.rsqrt(jnp.mean(x * x, axis=-1, keepdims=True) + EPS) * w


def _head_slices(n):
    return [slice(h * HEAD_DIM, (h + 1) * HEAD_DIM) for h in range(n)]


def _norm_mm_kernel(x_ref, g_ref, w_ref, o_ref, hn_ref):
    @pl.when(pl.program_id(1) == 0)
    def _():
        hn_ref[...] = _rms(x_ref[...], g_ref[...]).astype(BF16)

    o_ref[...] = _dot_nt(hn_ref[...], w_ref[...])


def _norm_matmul(x, g, wt, *, tm, tn):
    m, d = x.shape
    n = wt.shape[0]
    return pl.pallas_call(
        _norm_mm_kernel,
        out_shape=jax.ShapeDtypeStruct((m, n), F32),
        grid=(m // tm, n // tn),
        in_specs=[
            pl.BlockSpec((tm, d), lambda i, j: (i, 0)),
            pl.BlockSpec((1, d), lambda i, j: (0, 0)),
            pl.BlockSpec((tn, d), lambda i, j: (j, 0)),
        ],
        out_specs=pl.BlockSpec((tm, tn), lambda i, j: (i, j)),
        scratch_shapes=[pltpu.VMEM((tm, d), BF16)],
        compiler_params=_cparams("parallel", "arbitrary"),
        name="norm_in_proj",
    )(x, g, wt)


def _dn_pre_kernel(x_ref, halo_ref, st_ref, w_ref, q_ref, k_ref, v_ref, newst_ref, xc_ref, *, tt):
    t = pl.program_id(1)
    xc_ref[pl.ds(8, tt), :] = x_ref[0]

    @pl.when(t == 0)
    def _():
        xc_ref[5:8, :] = st_ref[0]

    @pl.when(t > 0)
    def _():
        xc_ref[0:8, :] = halo_ref[0]

    for part in range(3):
        for h in range(H_DN):
            cs = slice(part * DN_W + h * HEAD_DIM, part * DN_W + (h + 1) * HEAD_DIM)
            hs = slice(h * HEAD_DIM, (h + 1) * HEAD_DIM)
            y = xc_ref[pl.ds(5, tt), cs] * w_ref[0:1, cs]
            for i in range(1, DN_CONV):
                y = y + xc_ref[pl.ds(5 + i, tt), cs] * w_ref[i:i + 1, cs]
            y = _silu(y)
            if part == 2:
                v_ref[0, :, hs] = y
            else:
                yn = y * lax.rsqrt(jnp.sum(y * y, axis=-1, keepdims=True) + EPS)
                if part == 0:
                    q_ref[0, :, hs] = yn * SCALE
                else:
                    k_ref[0, :, hs] = yn
    newst_ref[0] = xc_ref[pl.ds(tt + 5, DN_CONV - 1), :]


def _dn_pre(proj3, state, w, *, tt):
    b, t, _ = proj3.shape
    c = 3 * DN_W
    hb = tt // 8
    out_shape = [jax.ShapeDtypeStruct((b, t, DN_W), F32)] * 3 + [jax.ShapeDtypeStruct((b, DN_CONV - 1, c), F32)]
    qkv_spec = pl.BlockSpec((1, tt, DN_W), lambda i, j: (i, j, 0))
    return pl.pallas_call(
        functools.partial(_dn_pre_kernel, tt=tt),
        out_shape=out_shape,
        grid=(b, t // tt),
        in_specs=[
            pl.BlockSpec((1, tt, c), lambda i, j: (i, j, 0)),
            pl.BlockSpec((1, 8, c), lambda i, j: (i, jnp.maximum(j * hb - 1, 0), 0)),
            pl.BlockSpec((1, DN_CONV - 1, c), lambda i, j: (i, 0, 0)),
            pl.BlockSpec((DN_CONV, c), lambda i, j: (0, 0)),
        ],
        out_specs=[qkv_spec, qkv_spec, qkv_spec, pl.BlockSpec((1, DN_CONV - 1, c), lambda i, j: (i, 0, 0))],
        scratch_shapes=[pltpu.VMEM((tt + 8, c), F32)],
        compiler_params=_cparams("parallel", "arbitrary"),
        name="dn_conv_prep",
    )(proj3, proj3, state, w)


def _scan_rows(x, row, limit):
    d = 1
    while d < limit:
        x = x + jnp.where(row >= d, pltpu.roll(x, d, axis=0), 0.0)
        d *= 2
    return x


def _gates_kernel(x_ref, bias_ref, alog_ref, gates_ref, cum_ref, gcl_ref, carry_ref, *, tt, chunk):
    t = pl.program_id(1)

    @pl.when(t == 0)
    def _():
        carry_ref[...] = jnp.zeros_like(carry_ref)

    x = x_ref[0] + bias_ref[...]
    lane = lax.broadcasted_iota(jnp.int32, x.shape, 1)
    row = lax.broadcasted_iota(jnp.int32, x.shape, 0)
    lp = jnp.log1p(jnp.exp(-jnp.abs(x)))
    softplus = jnp.maximum(x, 0.0) + lp
    logsig = jnp.minimum(x, 0.0) - lp
    decay = -jnp.exp(alog_ref[...]) * softplus
    out = jnp.where(lane < H_DN, _sigmoid(x),
                    jnp.where(lane < 2 * H_DN, decay,
                              jnp.where(lane < 2 * H_DN + H_FOX, logsig, 0.0)))
    gates_ref[0] = out
    c = _scan_rows(out, row, tt) + carry_ref[...]
    cum_ref[0] = c
    carry_ref[...] = c[tt - 1:tt, :]
    gcl_ref[0] = _scan_rows(out, row % chunk, min(chunk, tt))


def _gates(proj3, bias, alog, *, tt):
    b, t, _ = proj3.shape
    spec = pl.BlockSpec((1, tt, LANE), lambda i, j: (i, j, 0))
    return pl.pallas_call(
        functools.partial(_gates_kernel, tt=tt, chunk=DN_CHUNK),
        out_shape=[jax.ShapeDtypeStruct((b, t, LANE), F32)] * 3,
        grid=(b, t // tt),
        in_specs=[
            pl.BlockSpec((1, tt, LANE), lambda i, j: (i, j, BLK_GATE)),
            pl.BlockSpec((1, LANE), lambda i, j: (0, 0)),
            pl.BlockSpec((1, LANE), lambda i, j: (0, 0)),
        ],
        out_specs=[spec, spec, spec],
        scratch_shapes=[pltpu.VMEM((1, LANE), F32)],
        compiler_params=_cparams("parallel", "arbitrary"),
        name="gate_act",
    )(proj3, bias, alog)


def _gdn_kernel(q_ref, k_ref, v_ref, z_ref, gates_ref, gcl_ref, gclt_ref, s0_ref, nw_ref, o_ref, s_ref, *, c, nck):
    @pl.when(pl.program_id(1) == 0)
    def _():
        s_ref[...] = s0_ref[...]

    ii = lax.broadcasted_iota(jnp.int32, (c, c), 0)
    jj = lax.broadcasted_iota(jnp.int32, (c, c), 1)
    lower = ii >= jj
    strict = ii > jj
    eye = jnp.where(ii == jj, 1.0, 0.0).astype(F32)
    n_sq = c.bit_length() - 2
    heads = range(H_DN)
    hsl = _head_slices(H_DN)
    pairs = [(ck, h) for ck in range(nck) for h in heads]
    rows = [slice(ck * c, (ck + 1) * c) for ck in range(nck)]
    q = {p: q_ref[0, rows[p[0]], hsl[p[1]]] for p in pairs}
    k = {p: k_ref[0, rows[p[0]], hsl[p[1]]] for p in pairs}
    v = {p: v_ref[0, rows[p[0]], hsl[p[1]]] for p in pairs}
    beta = {p: gates_ref[0, rows[p[0]], p[1]:p[1] + 1] for p in pairs}
    gc = {p: gcl_ref[0, rows[p[0]], H_DN + p[1]:H_DN + p[1] + 1] for p in pairs}
    gr = {p: gclt_ref[0, p[0], p[1]:p[1] + 1, :] for p in pairs}
    gl = {p: gr[p][:, c - 1:c] for p in pairs}
    gamma = {p: jnp.exp(jnp.where(lower, gc[p] - gr[p], -jnp.inf)) for p in pairs}
    kb = {p: k[p] * beta[p] for p in pairs}
    kbf = {p: k[p].astype(BF16) for p in pairs}
    kk = {p: _dot_nt(kb[p].astype(BF16), kbf[p]) for p in pairs}
    mm = {p: jnp.where(strict, -(kk[p] * gamma[p]), 0.0) for p in pairs}
    tinv = {p: eye + mm[p] for p in pairs}
    ps = {p: _split(mm[p]) for p in pairs}
    for _ in range(n_sq):
        ps = {p: _split(_dot3(ps[p], ps[p])) for p in pairs}
        tinv = {p: tinv[p] + _dot3(_split(tinv[p]), ps[p]) for p in pairs}
    egc = {p: jnp.exp(gc[p]) for p in pairs}
    rhs = {p: _split(jnp.concatenate([kb[p] * egc[p], v[p] * beta[p]], axis=1)) for p in pairs}
    wu = {p: _dot3(_split(tinv[p]), rhs[p]) for p in pairs}
    qk = {p: (_dot_nt(q[p].astype(BF16), kbf[p]) * gamma[p]).astype(BF16) for p in pairs}
    wb = {p: wu[p][:, :HEAD_DIM].astype(BF16) for p in pairs}
    qg = {p: (q[p] * egc[p]).astype(BF16) for p in pairs}
    k_dec = {p: (k[p] * jnp.exp(gl[p] - gc[p])).astype(BF16) for p in pairs}
    s = [s_ref[0, h] for h in heads]
    for ck in range(nck):
        sb = [x.astype(BF16) for x in s]
        v_new = [wu[ck, h][:, HEAD_DIM:] - _dot(wb[ck, h], sb[h]) for h in heads]
        vnb = [x.astype(BF16) for x in v_new]
        s = [s[h] * jnp.exp(gl[ck, h]) + _dot_tn(k_dec[ck, h], vnb[h]) for h in heads]
        for h in heads:
            o = _dot(qg[ck, h], sb[h]) + _dot(qk[ck, h], vnb[h])
            o_ref[0, rows[ck], hsl[h]] = (_rms(o, nw_ref[...]) * _silu(z_ref[0, rows[ck], hsl[h]])).astype(o_ref.dtype)
    for h in heads:
        s_ref[0, h] = s[h]


def _gdn(q, k, v, z, z_blk, gates, gcl, gclt, s0, nw, *, nck):
    b, t, _ = q.shape
    c = DN_CHUNK
    tc = nck * c
    blk = pl.BlockSpec((1, tc, DN_W), lambda i, j: (i, j, 0))
    gblk = pl.BlockSpec((1, tc, LANE), lambda i, j: (i, j, 0))
    sblk = pl.BlockSpec((1, H_DN, HEAD_DIM, HEAD_DIM), lambda i, j: (i, 0, 0, 0))
    return pl.pallas_call(
        functools.partial(_gdn_kernel, c=c, nck=nck),
        out_shape=[jax.ShapeDtypeStruct((b, t, DN_W), BF16), jax.ShapeDtypeStruct(s0.shape, F32)],
        grid=(b, t // tc),
        in_specs=[
            blk, blk, blk,
            pl.BlockSpec((1, tc, DN_W), lambda i, j: (i, j, z_blk)),
            gblk, gblk,
            pl.BlockSpec((1, nck, 8, c), lambda i, j: (i, j, 0, 0)),
            sblk,
            pl.BlockSpec((1, HEAD_DIM), lambda i, j: (0, 0)),
        ],
        out_specs=[blk, sblk],
        compiler_params=_cparams("parallel", "arbitrary"),
        name="gated_delta_rule",
    )(q, k, v, z, gates, gcl, gclt, s0, nw)


def _fox_kernel(q_ref, k_ref, v_ref, ct_ref, nw_ref, o_ref, qb_ref, m_ref, l_ref, acc_ref, *, tq):
    qi = pl.program_id(1)
    heads = range(H_FOX)
    hsl = _head_slices(H_FOX)
    row = lax.broadcasted_iota(jnp.int32, (tq, tq), 0)
    col = lax.broadcasted_iota(jnp.int32, (tq, tq), 1)
    ones = jnp.ones((tq, HEAD_DIM), BF16)
    qb_ref[...] = q_ref[0].astype(BF16)
    m_ref[...] = jnp.full_like(m_ref, -jnp.inf)
    l_ref[...] = jnp.zeros_like(l_ref)
    acc_ref[...] = jnp.zeros_like(acc_ref)

    def step(j, diagonal):
        ks = pl.ds(pl.multiple_of(j * tq, tq), tq)
        for g0 in range(0, H_FOX, FOX_HEAD_GROUP):
            grp = range(g0, g0 + FOX_HEAD_GROUP)
            s = {h: _dot_nt(qb_ref[:, hsl[h]], k_ref[0, ks, hsl[h]].astype(BF16)) * SCALE - ct_ref[0, j, h:h + 1, :]
                 for h in grp}
            if diagonal:
                s = {h: jnp.where(col <= row, s[h], -jnp.inf) for h in grp}
            m_old = {h: m_ref[h] for h in grp}
            m_new = {h: jnp.maximum(m_old[h], jnp.max(s[h], axis=-1, keepdims=True)) for h in grp}
            p = {h: jnp.exp(s[h] - m_new[h]) for h in grp}
            a = {h: jnp.exp(m_old[h] - m_new[h]) for h in grp}
            for h in grp:
                pb = p[h].astype(BF16)
                m_ref[h] = m_new[h]
                l_ref[h] = a[h] * l_ref[h] + _dot(pb, ones)
                acc_ref[h] = a[h] * acc_ref[h] + _dot(pb, v_ref[0, ks, hsl[h]].astype(BF16))

    def body(j, carry):
        step(j, False)
        return carry

    lax.fori_loop(0, qi, body, 0)
    step(qi, True)
    for h in heads:
        o_ref[0, :, hsl[h]] = _rms(acc_ref[h] / l_ref[h], nw_ref[...]).astype(o_ref.dtype)


def _fox_prompt(proj3, ct, nw, *, tq):
    b, t, _ = proj3.shape
    return pl.pallas_call(
        functools.partial(_fox_kernel, tq=tq),
        out_shape=jax.ShapeDtypeStruct((b, t, FOX_W), BF16),
        grid=(b, t // tq),
        in_specs=[
            pl.BlockSpec((1, tq, FOX_W), lambda i, j: (i, j, BLK_FQ // H_FOX)),
            pl.BlockSpec((1, t, FOX_W), lambda i, j: (i, 0, BLK_FK // H_FOX)),
            pl.BlockSpec((1, t, FOX_W), lambda i, j: (i, 0, BLK_FV // H_FOX)),
            pl.BlockSpec((1, t // tq, 8, tq), lambda i, j: (i, 0, 0, 0)),
            pl.BlockSpec((1, HEAD_DIM), lambda i, j: (0, 0)),
        ],
        out_specs=pl.BlockSpec((1, tq, FOX_W), lambda i, j: (i, j, 0)),
        scratch_shapes=[
            pltpu.VMEM((tq, FOX_W), BF16),
            pltpu.VMEM((H_FOX, tq, 1), F32),
            pltpu.VMEM((H_FOX, tq, HEAD_DIM), F32),
            pltpu.VMEM((H_FOX, tq, HEAD_DIM), F32),
        ],
        compiler_params=_cparams("parallel", "arbitrary"),
        name="fox_prompt",
    )(proj3, proj3, proj3, ct, nw)


def _suffix_incl(lg, tri):
    hi, lo = _split(lg)
    return _dot(hi, tri) + _dot(lo, tri)


def _sb_kernel(q_ref, k_ref, v_ref, nw_ref, o_ref, qb_ref, r_ref, acc_ref, *, tq):
    qi = pl.program_id(1)
    heads = range(H_SB)
    hsl = _head_slices(H_SB)
    row = lax.broadcasted_iota(jnp.int32, (tq, tq), 0)
    col = lax.broadcasted_iota(jnp.int32, (tq, tq), 1)
    tri = jnp.where(row >= col, 1.0, 0.0).astype(BF16)
    qb_ref[...] = q_ref[0].astype(BF16)
    r_ref[...] = jnp.zeros_like(r_ref)
    acc_ref[...] = jnp.zeros_like(acc_ref)

    def step(jr, diagonal):
        ks = pl.ds(pl.multiple_of((qi - jr) * tq, tq), tq)
        z = [_dot_nt(qb_ref[:, hsl[h]], k_ref[0, ks, hsl[h]].astype(BF16)) * SCALE for h in heads]
        lg = [_neg_softplus(x) for x in z]
        if diagonal:
            lg = [jnp.where(col < row, x, 0.0) for x in lg]
        incl = [_suffix_incl(x, tri) for x in lg]
        a = [jnp.exp(z[h] + incl[h] + r_ref[h]) for h in heads]
        if diagonal:
            a = [jnp.where(col < row, x, 0.0) for x in a]
        for h in heads:
            acc_ref[h] += _dot(a[h].astype(BF16), v_ref[0, ks, hsl[h]].astype(BF16))
            r_ref[h] += incl[h][:, 0:1]

    def body(jr, carry):
        step(jr, False)
        return carry

    step(0, True)
    lax.fori_loop(1, qi + 1, body, 0)
    for h in heads:
        o_ref[0, :, hsl[h]] = _rms(acc_ref[h], nw_ref[...]).astype(o_ref.dtype)


def _sb_prompt(proj3, nw, *, tq):
    b, t, _ = proj3.shape
    return pl.pallas_call(
        functools.partial(_sb_kernel, tq=tq),
        out_shape=jax.ShapeDtypeStruct((b, t, SB_W), BF16),
        grid=(b, t // tq),
        in_specs=[
            pl.BlockSpec((1, tq, SB_W), lambda i, j: (i, j, BLK_SQ // H_SB)),
            pl.BlockSpec((1, t, SB_W), lambda i, j: (i, 0, BLK_SK // H_SB)),
            pl.BlockSpec((1, t, SB_W), lambda i, j: (i, 0, BLK_SV // H_SB)),
            pl.BlockSpec((1, HEAD_DIM), lambda i, j: (0, 0)),
        ],
        out_specs=pl.BlockSpec((1, tq, SB_W), lambda i, j: (i, j, 0)),
        scratch_shapes=[
            pltpu.VMEM((tq, SB_W), BF16),
            pltpu.VMEM((H_SB, tq, 1), F32),
            pltpu.VMEM((H_SB, tq, HEAD_DIM), F32),
        ],
        compiler_params=_cparams("parallel", "arbitrary"),
        name="sb_prompt",
    )(proj3, proj3, proj3, nw)


def _page_rows(page_ref, head_major=False):
    x = page_ref[...] if head_major else jnp.swapaxes(page_ref[...], 0, 1)
    return jnp.concatenate([x[h] for h in range(x.shape[0])], axis=1).astype(BF16)


def _block_diag_q(q_ref, nh, t):
    zero = jnp.zeros((t, HEAD_DIM), F32)
    rows = [jnp.concatenate([q_ref[0, :, h * HEAD_DIM:(h + 1) * HEAD_DIM] if j == h else zero
                             for j in range(nh)], axis=1) for h in range(nh)]
    return jnp.concatenate(rows, axis=0).astype(BF16)


def _rows_per_head(x, nh, t):
    return jnp.concatenate([jnp.broadcast_to(x[h:h + 1, :], (t, x.shape[1])) for h in range(nh)], axis=0)


def _scores(qblk, keys):
    return _dot_nt(qblk, keys) * SCALE


def _weighted_values(w, values, nh, t):
    full = _dot(w.astype(BF16), values)
    return jnp.concatenate([full[h * t:(h + 1) * t, h * HEAD_DIM:(h + 1) * HEAD_DIM] for h in range(nh)], axis=0)


def _fox_dec_kernel(pt_ref, q_ref, kn_ref, vn_ref, cn_ref, nw_ref, *rest, g, t):
    kp, vp, lp = rest[:g], rest[g:2 * g], rest[2 * g:3 * g]
    o_ref = rest[3 * g]
    m_ref, l_ref, acc_ref, off_ref, kpad_ref, vpad_ref = rest[3 * g + 1:]
    nh = H_FOX
    hsl = _head_slices(nh)
    p = pl.program_id(1)

    @pl.when(p == 0)
    def _():
        m_ref[...] = jnp.full_like(m_ref, -jnp.inf)
        l_ref[...] = jnp.zeros_like(l_ref)
        acc_ref[...] = jnp.zeros_like(acc_ref)
        off_ref[...] = jnp.zeros_like(off_ref)

    qblk = _block_diag_q(q_ref, nh, t)
    ii = lax.broadcasted_iota(jnp.int32, (PAGE, PAGE), 0)
    jj = lax.broadcasted_iota(jnp.int32, (PAGE, PAGE), 1)
    upper = jnp.where(ii <= jj, 1.0, 0.0).astype(F32)

    def update(s, weighted_values):
        m = m_ref[...]
        m_new = jnp.maximum(m, jnp.max(s, axis=-1, keepdims=True))
        pr = jnp.exp(s - m_new)
        a = jnp.exp(m - m_new)
        l_ref[...] = a * l_ref[...] + jnp.sum(pr, axis=-1, keepdims=True)
        acc_ref[...] = a * acc_ref[...] + weighted_values(pr)
        m_ref[...] = m_new

    cw_all = _dot_hi(jnp.concatenate([lp[i][...] for i in range(g)], axis=0), upper)
    off = off_ref[...]
    bias = []
    for i in range(g):
        cw = cw_all[8 * i:8 * (i + 1)]
        bias.append(_rows_per_head(off + cw, nh, t))
        off = off + cw[:, PAGE - 1:PAGE]
    off_ref[...] = off
    keys = jnp.concatenate([_page_rows(kp[i], True) for i in range(g)], axis=0)
    values = jnp.concatenate([_page_rows(vp[i], True) for i in range(g)], axis=0)
    update(_scores(qblk, keys) - jnp.concatenate(bias, axis=1), lambda pr: _weighted_values(pr, values, nh, t))

    @pl.when(p == pl.num_programs(1) - 1)
    def _():
        kpad_ref[...] = jnp.zeros_like(kpad_ref)
        vpad_ref[...] = jnp.zeros_like(vpad_ref)
        kpad_ref[0:t, :] = kn_ref[0]
        vpad_ref[0:t, :] = vn_ref[0]
        c = off_ref[...] + cn_ref[0]
        s = _scores(qblk, kpad_ref[...].astype(BF16)) - _rows_per_head(c, nh, t)
        r = lax.broadcasted_iota(jnp.int32, s.shape, 0)
        cc = lax.broadcasted_iota(jnp.int32, s.shape, 1)
        s = jnp.where(cc <= r % t, s, -jnp.inf)
        update(s, lambda pr: _weighted_values(pr, vpad_ref[...].astype(BF16), nh, t))
        o = acc_ref[...] / l_ref[...]
        for h in range(nh):
            o_ref[0, :, hsl[h]] = _rms(o[h * t:(h + 1) * t], nw_ref[...]).astype(o_ref.dtype)


def _fox_dec(page_table, proj3, cnt, nw, k_pool, v_pool, lf_pool, layer, *, g):
    b, t, _ = proj3.shape
    npg = page_table.shape[1]
    nh = H_FOX

    def page_spec(shape, i):
        return pl.BlockSpec((None, None) + shape, lambda bi, p, pt, i=i: (layer, pt[bi, p * g + i], 0, 0, 0)[:2 + len(shape)])

    in_specs = [
        pl.BlockSpec((1, t, FOX_W), lambda bi, p, pt: (bi, 0, BLK_FQ // nh)),
        pl.BlockSpec((1, t, FOX_W), lambda bi, p, pt: (bi, 0, BLK_FK // nh)),
        pl.BlockSpec((1, t, FOX_W), lambda bi, p, pt: (bi, 0, BLK_FV // nh)),
        pl.BlockSpec((1, 8, PAGE), lambda bi, p, pt: (bi, 0, 0)),
        pl.BlockSpec((1, HEAD_DIM), lambda bi, p, pt: (0, 0)),
    ]
    in_specs += [page_spec((nh, PAGE, HEAD_DIM), i) for i in range(g)]
    in_specs += [page_spec((nh, PAGE, HEAD_DIM), i) for i in range(g)]
    in_specs += [page_spec((8, PAGE), i) for i in range(g)]
    grid_spec = pltpu.PrefetchScalarGridSpec(
        num_scalar_prefetch=1,
        grid=(b, npg // g),
        in_specs=in_specs,
        out_specs=pl.BlockSpec((1, t, FOX_W), lambda bi, p, pt: (bi, 0, 0)),
        scratch_shapes=[
            pltpu.VMEM((nh * t, 1), F32),
            pltpu.VMEM((nh * t, 1), F32),
            pltpu.VMEM((nh * t, HEAD_DIM), F32),
            pltpu.VMEM((8, 1), F32),
            pltpu.VMEM((PAGE, FOX_W), F32),
            pltpu.VMEM((PAGE, FOX_W), F32),
        ],
    )
    return pl.pallas_call(
        functools.partial(_fox_dec_kernel, g=g, t=t),
        out_shape=jax.ShapeDtypeStruct((b, t, FOX_W), BF16),
        grid_spec=grid_spec,
        compiler_params=_cparams("parallel", "arbitrary"),
        name="fox_paged",
    )(page_table, proj3, proj3, proj3, cnt, nw, *([k_pool] * g), *([v_pool] * g), *([lf_pool] * g))


def _sb_dec_kernel(pt_ref, q_ref, kn_ref, vn_ref, nw_ref, *rest, g, t):
    kp, vp = rest[:g], rest[g:2 * g]
    o_ref = rest[2 * g]
    rsum_ref, acc_ref, kpad_ref, vpad_ref = rest[2 * g + 1:]
    nh = H_SB
    hsl = _head_slices(nh)
    p = pl.program_id(1)
    ii = lax.broadcasted_iota(jnp.int32, (PAGE, PAGE), 0)
    jj = lax.broadcasted_iota(jnp.int32, (PAGE, PAGE), 1)
    tri = jnp.where(ii >= jj, 1.0, 0.0).astype(BF16)
    qblk = _block_diag_q(q_ref, nh, t)

    @pl.when(p == 0)
    def _():
        kpad_ref[...] = jnp.zeros_like(kpad_ref)
        vpad_ref[...] = jnp.zeros_like(vpad_ref)
        kpad_ref[0:t, :] = kn_ref[0]
        vpad_ref[0:t, :] = vn_ref[0]
        z = _scores(qblk, kpad_ref[...].astype(BF16))
        r = lax.broadcasted_iota(jnp.int32, z.shape, 0)
        cc = lax.broadcasted_iota(jnp.int32, z.shape, 1)
        valid = cc < r % t
        incl = _suffix_incl(jnp.where(valid, _neg_softplus(z), 0.0), tri)
        a = jnp.where(valid, jnp.exp(z + incl), 0.0)
        rsum_ref[...] = incl[:, 0:1]
        acc_ref[...] = _weighted_values(a, vpad_ref[...].astype(BF16), nh, t)

    keys = jnp.concatenate([_page_rows(kp[i]) for i in range(g)], axis=0)
    z = _scores(qblk, keys)
    lg = _neg_softplus(z)
    incl_rows = _suffix_incl(jnp.concatenate([lg[:, i * PAGE:(i + 1) * PAGE] for i in range(g)], axis=0), tri)
    rsum = rsum_ref[...]
    weights = []
    for i in range(g):
        incl = incl_rows[i * nh * t:(i + 1) * nh * t]
        weights.append(jnp.exp(z[:, i * PAGE:(i + 1) * PAGE] + incl + rsum))
        rsum = rsum + incl[:, 0:1]
    rsum_ref[...] = rsum
    values = jnp.concatenate([_page_rows(vp[i]) for i in range(g)], axis=0)
    acc_ref[...] += _weighted_values(jnp.concatenate(weights, axis=1), values, nh, t)

    @pl.when(p == pl.num_programs(1) - 1)
    def _():
        acc = acc_ref[...]
        for h in range(nh):
            o_ref[0, :, hsl[h]] = _rms(acc[h * t:(h + 1) * t], nw_ref[...]).astype(o_ref.dtype)


def _sb_dec(page_table, proj3, nw, k_pool, v_pool, layer, *, g):
    b, t, _ = proj3.shape
    npg = page_table.shape[1]
    nh = H_SB

    def page_spec(i):
        return pl.BlockSpec((None, None, PAGE, nh, HEAD_DIM),
                            lambda bi, p, pt, i=i: (layer, pt[bi, npg - 1 - (p * g + i)], 0, 0, 0))

    in_specs = [
        pl.BlockSpec((1, t, SB_W), lambda bi, p, pt: (bi, 0, BLK_SQ // nh)),
        pl.BlockSpec((1, t, SB_W), lambda bi, p, pt: (bi, 0, BLK_SK // nh)),
        pl.BlockSpec((1, t, SB_W), lambda bi, p, pt: (bi, 0, BLK_SV // nh)),
        pl.BlockSpec((1, HEAD_DIM), lambda bi, p, pt: (0, 0)),
    ]
    in_specs += [page_spec(i) for i in range(g)] * 2
    grid_spec = pltpu.PrefetchScalarGridSpec(
        num_scalar_prefetch=1,
        grid=(b, npg // g),
        in_specs=in_specs,
        out_specs=pl.BlockSpec((1, t, SB_W), lambda bi, p, pt: (bi, 0, 0)),
        scratch_shapes=[
            pltpu.VMEM((nh * t, 1), F32),
            pltpu.VMEM((nh * t, HEAD_DIM), F32),
            pltpu.VMEM((PAGE, SB_W), F32),
            pltpu.VMEM((PAGE, SB_W), F32),
        ],
    )
    return pl.pallas_call(
        functools.partial(_sb_dec_kernel, g=g, t=t),
        out_shape=jax.ShapeDtypeStruct((b, t, SB_W), BF16),
        grid_spec=grid_spec,
        compiler_params=_cparams("parallel", "arbitrary"),
        name="sb_paged",
    )(page_table, proj3, proj3, proj3, nw, *([k_pool] * g), *([v_pool] * g))


def _head_major_kernel(*refs, nh):
    x_ref, o_ref = refs[0], refs[-1]
    for h in range(nh):
        o_ref[0, h] = x_ref[0, :, h * HEAD_DIM:(h + 1) * HEAD_DIM]


def _head_major_rows(proj3, col_blk, nh, layer, depth, prev, *, tt):
    b, t, _ = proj3.shape
    w = nh * HEAD_DIM
    in_specs = [pl.BlockSpec((1, tt, w), lambda i, j: (i, j, col_blk * LANE // w))]
    args = [proj3]
    aliases = {}
    if prev is not None:
        in_specs.append(pl.BlockSpec(memory_space=pl.ANY))
        args.append(prev)
        aliases = {1: 0}
    return pl.pallas_call(
        functools.partial(_head_major_kernel, nh=nh),
        out_shape=jax.ShapeDtypeStruct((depth, b, nh, t, HEAD_DIM), F32),
        grid=(b, t // tt),
        in_specs=in_specs,
        out_specs=pl.BlockSpec((None, 1, nh, tt, HEAD_DIM), lambda i, j: (layer, i, 0, j, 0)),
        input_output_aliases=aliases,
        compiler_params=_cparams("parallel", "parallel"),
        name="head_major_rows",
    )(*args)


def _out_proj_kernel(x_ref, a_ref, b_ref, c_ref, w_ref, g_ref, o_ref):
    acc = _dot(a_ref[...], w_ref[0:DN_W, :])
    acc += _dot(b_ref[...], w_ref[DN_W:DN_W + FOX_W, :])
    acc += _dot(c_ref[...], w_ref[DN_W + FOX_W:, :])
    o_ref[...] = x_ref[...] + _rms(acc, g_ref[...])


def _out_proj(x, oa, ob, oc, w, layer, g, *, tm):
    m, d = x.shape
    return pl.pallas_call(
        _out_proj_kernel,
        out_shape=jax.ShapeDtypeStruct((m, d), F32),
        grid=(m // tm,),
        in_specs=[
            pl.BlockSpec((tm, d), lambda i: (i, 0)),
            pl.BlockSpec((tm, DN_W), lambda i: (i, 0)),
            pl.BlockSpec((tm, FOX_W), lambda i: (i, 0)),
            pl.BlockSpec((tm, SB_W), lambda i: (i, 0)),
            pl.BlockSpec((None,) + w.shape[1:], lambda i: (layer, 0, 0)),
            pl.BlockSpec((1, d), lambda i: (0, 0)),
        ],
        out_specs=pl.BlockSpec((tm, d), lambda i: (i, 0)),
        compiler_params=_cparams("parallel"),
        name="out_proj",
    )(x, oa, ob, oc, w, g)


def _ffn_tail(c, x_ref, gpost_ref, acc_ref, o_ref):
    @pl.when(c == pl.num_programs(1) - 1)
    def _():
        o_ref[...] = x_ref[...] + _rms(acc_ref[...], gpost_ref[...])


def _ffn_prompt_kernel(x_ref, halo_ref, gpre_ref, wg_ref, wu_ref, cg_ref, cu_ref, wd_ref, gpost_ref,
                       o_ref, st_ref, hn_ref, acc_ref, ug_ref, uu_ref, *, tm, tiles_per_seq):
    i = pl.program_id(0)
    c = pl.program_id(1)

    @pl.when(c == 0)
    def _():
        hn_ref[pl.ds(8, tm), :] = _rms(x_ref[...], gpre_ref[...]).astype(BF16)
        keep = jnp.where(i % tiles_per_seq == 0, 0.0, 1.0)
        hn_ref[0:8, :] = (_rms(halo_ref[...], gpre_ref[...]) * keep).astype(BF16)
        acc_ref[...] = jnp.zeros_like(acc_ref)

    hn = hn_ref[...]
    ug_ref[...] = _dot(hn, wg_ref[...])
    uu_ref[...] = _dot(hn, wu_ref[...])

    def conv(u_ref, w_ref):
        y = u_ref[pl.ds(8 - (FFN_CONV - 1), tm), :] * w_ref[0:1, :]
        for k in range(1, FFN_CONV):
            y = y + u_ref[pl.ds(8 - (FFN_CONV - 1) + k, tm), :] * w_ref[k:k + 1, :]
        return y

    act = _silu(conv(ug_ref, cg_ref)) * conv(uu_ref, cu_ref)
    acc_ref[...] += _dot(act.astype(BF16), wd_ref[...])
    st_ref[0, 0] = ug_ref[pl.ds(tm + 8 - (FFN_CONV - 1), FFN_CONV - 1), :]
    st_ref[0, 1] = uu_ref[pl.ds(tm + 8 - (FFN_CONV - 1), FFN_CONV - 1), :]
    _ffn_tail(c, x_ref, gpost_ref, acc_ref, o_ref)


def _ffn_prompt(x, seq_len, gpre, w_up, conv_w, w_down, layer, gpost, *, tm, tc):
    m, d = x.shape
    dff = w_down.shape[1]
    nc = dff // tc
    tps = seq_len // tm
    hb = tm // 8
    return pl.pallas_call(
        functools.partial(_ffn_prompt_kernel, tm=tm, tiles_per_seq=tps),
        out_shape=[jax.ShapeDtypeStruct((m, d), F32),
                   jax.ShapeDtypeStruct((m // tm, 2, FFN_CONV - 1, dff), F32)],
        grid=(m // tm, nc),
        in_specs=[
            pl.BlockSpec((tm, d), lambda i, c: (i, 0)),
            pl.BlockSpec((8, d), lambda i, c: (jnp.maximum(i * hb - 1, 0), 0)),
            pl.BlockSpec((1, d), lambda i, c: (0, 0)),
            pl.BlockSpec((None, d, tc), lambda i, c: (layer, 0, c)),
            pl.BlockSpec((None, d, tc), lambda i, c: (layer, 0, nc + c)),
            pl.BlockSpec((FFN_CONV, tc), lambda i, c: (0, c)),
            pl.BlockSpec((FFN_CONV, tc), lambda i, c: (0, nc + c)),
            pl.BlockSpec((None, tc, d), lambda i, c: (layer, c, 0)),
            pl.BlockSpec((1, d), lambda i, c: (0, 0)),
        ],
        out_specs=[
            pl.BlockSpec((tm, d), lambda i, c: (i, 0)),
            pl.BlockSpec((1, 2, FFN_CONV - 1, tc), lambda i, c: (i, 0, 0, c)),
        ],
        scratch_shapes=[
            pltpu.VMEM((tm + 8, d), BF16),
            pltpu.VMEM((tm, d), F32),
            pltpu.VMEM((tm + 8, tc), F32),
            pltpu.VMEM((tm + 8, tc), F32),
        ],
        compiler_params=_cparams("parallel", "arbitrary"),
        name="conv_ffn_prompt",
    )(x, x, gpre, w_up, w_up, conv_w, conv_w, w_down, gpost)


def _ffn_dec_kernel(x_ref, gpre_ref, wg_ref, wu_ref, cg_ref, cu_ref, wd_ref, gpost_ref,
                    s0g_ref, s1g_ref, s0u_ref, s1u_ref,
                    o_ref, ugo_ref, uuo_ref, hn_ref, acc_ref, ug_ref, uu_ref, *, tm, t):
    c = pl.program_id(1)

    @pl.when(c == 0)
    def _():
        hn_ref[...] = _rms(x_ref[...], gpre_ref[...]).astype(BF16)
        acc_ref[...] = jnp.zeros_like(acc_ref)
        ug_ref[0:8, :] = jnp.zeros((8, ug_ref.shape[1]), F32)
        uu_ref[0:8, :] = jnp.zeros((8, uu_ref.shape[1]), F32)

    hn = hn_ref[...]
    ug = _dot(hn, wg_ref[...])
    uu = _dot(hn, wu_ref[...])
    ugo_ref[...] = ug
    uuo_ref[...] = uu
    ug_ref[pl.ds(8, tm), :] = ug
    uu_ref[pl.ds(8, tm), :] = uu
    pos = lax.broadcasted_iota(jnp.int32, ug.shape, 0) % t

    def conv(u, u_ref, s0_ref, s1_ref, w_ref):
        um1 = jnp.where(pos >= 1, u_ref[pl.ds(7, tm), :], s1_ref[...])
        um2 = jnp.where(pos >= 2, u_ref[pl.ds(6, tm), :], jnp.where(pos == 0, s0_ref[...], s1_ref[...]))
        return um2 * w_ref[0:1, :] + um1 * w_ref[1:2, :] + u * w_ref[2:3, :]

    act = _silu(conv(ug, ug_ref, s0g_ref, s1g_ref, cg_ref)) * conv(uu, uu_ref, s0u_ref, s1u_ref, cu_ref)
    acc_ref[...] += _dot(act.astype(BF16), wd_ref[...])
    _ffn_tail(c, x_ref, gpost_ref, acc_ref, o_ref)


def _ffn_dec(x, t, gpre, w_up, conv_w, w_down, layer, gpost, s0e, s1e, *, tc):
    m, d = x.shape
    dff = w_down.shape[1]
    nc = dff // tc
    tm = m
    st = lambda off: pl.BlockSpec((tm, tc), lambda i, c, off=off: (0, off * nc + c))
    return pl.pallas_call(
        functools.partial(_ffn_dec_kernel, tm=tm, t=t),
        out_shape=[jax.ShapeDtypeStruct((m, d), F32),
                   jax.ShapeDtypeStruct((m, dff), F32), jax.ShapeDtypeStruct((m, dff), F32)],
        grid=(1, nc),
        in_specs=[
            pl.BlockSpec((tm, d), lambda i, c: (0, 0)),
            pl.BlockSpec((1, d), lambda i, c: (0, 0)),
            pl.BlockSpec((None, d, tc), lambda i, c: (layer, 0, c)),
            pl.BlockSpec((None, d, tc), lambda i, c: (layer, 0, nc + c)),
            pl.BlockSpec((FFN_CONV, tc), lambda i, c: (0, c)),
            pl.BlockSpec((FFN_CONV, tc), lambda i, c: (0, nc + c)),
            pl.BlockSpec((None, tc, d), lambda i, c: (layer, c, 0)),
            pl.BlockSpec((1, d), lambda i, c: (0, 0)),
            st(0), st(0), st(1), st(1),
        ],
        out_specs=[
            pl.BlockSpec((tm, d), lambda i, c: (0, 0)),
            pl.BlockSpec((tm, tc), lambda i, c: (0, c)),
            pl.BlockSpec((tm, tc), lambda i, c: (0, c)),
        ],
        scratch_shapes=[
            pltpu.VMEM((tm, d), BF16),
            pltpu.VMEM((tm, d), F32),
            pltpu.VMEM((tm + 8, tc), F32),
            pltpu.VMEM((tm + 8, tc), F32),
        ],
        compiler_params=_cparams("arbitrary", "arbitrary"),
        name="conv_ffn_sample",
    )(x, gpre, w_up, w_up, conv_w, conv_w, w_down, gpost, s0e, s1e, s0e, s1e)


def _pack_w_in(w_in):
    d = w_in.shape[0]
    wt = w_in.T
    sizes = (3 * DN_W, H_DN, H_DN, DN_W, FOX_W, FOX_W, FOX_W, H_FOX, SB_W, SB_W, SB_W)
    parts, start = [], 0
    for s in sizes:
        parts.append(wt[start:start + s])
        start += s
    qkv, dn_b, dn_a, dn_z, fq, fk, fv, ff, sq, sk, sv = parts
    gate = jnp.concatenate([dn_b, dn_a, ff, jnp.zeros((2 * LANE - 2 * H_DN - H_FOX, d), w_in.dtype)], axis=0)
    return jnp.concatenate([qkv, dn_z, fq, fk, fv, gate, sq, sk, sv], axis=0).astype(BF16)


def _lane_row(pairs):
    row = jnp.zeros((LANE,), F32)
    for off, v in pairs:
        row = row.at[off:off + v.shape[0]].set(v.astype(F32))
    return row[None, :]


def _mixers_common(x3, lw, dn_state, tm, tt):
    b, t, d = x3.shape
    proj = _norm_matmul(x3.reshape(b * t, d), lw["norm_pre_mix"], lw["w_in"], tm=tm, tn=1024)
    proj3 = proj.reshape(b, t, PACK_W)
    q, k, v, dn_buf_new = _dn_pre(proj3, dn_state, lw["dn_conv"], tt=tt)
    gates, cum, gcl = _gates(proj3, lw["gate_bias"], lw["alog_row"], tt=tt)
    return proj3, q, k, v, dn_buf_new, gates, cum, gcl


def _gclt(gcl, c):
    b, t, _ = gcl.shape
    g = gcl[:, :, H_DN:2 * H_DN].reshape(b, t // c, c, H_DN)
    g = jnp.swapaxes(g, 2, 3)
    return jnp.pad(g, ((0, 0), (0, 0), (0, 8 - H_DN), (0, 0)))


def _new_rows(proj3, gates):
    b, t, _ = proj3.shape

    def cols(blk, w, nh):
        return proj3[:, :, blk * LANE:blk * LANE + w].reshape(b, t, nh, HEAD_DIM)

    return (cols(BLK_FK, FOX_W, H_FOX), cols(BLK_FV, FOX_W, H_FOX), gates[:, :, 2 * H_DN:2 * H_DN + H_FOX],
            cols(BLK_SK, SB_W, H_SB), cols(BLK_SV, SB_W, H_SB))


def _layer_prompt(x3, lw, layer, depth, fox_bufs):
    b, t, d = x3.shape
    tq = 256
    tm_ffn = 512
    zeros_dn = jnp.zeros((b, DN_CONV - 1, 3 * DN_W), F32)
    proj3, q, k, v, dn_buf_new, gates, cum, gcl = _mixers_common(x3, lw, zeros_dn, tm=1024, tt=256)
    s0 = jnp.zeros((b, H_DN, HEAD_DIM, HEAD_DIM), F32)
    o_a, s_new = _gdn(q, k, v, proj3, BLK_Z // H_DN, gates, gcl, _gclt(gcl, DN_CHUNK), s0, lw["dn_norm"],
                      nck=GDN_CHUNKS_PER_STEP)
    ct = cum[:, :, 2 * H_DN:2 * H_DN + H_FOX].reshape(b, t // tq, tq, H_FOX)
    ct = jnp.pad(jnp.swapaxes(ct, 2, 3), ((0, 0), (0, 0), (0, 8 - H_FOX), (0, 0)))
    o_b = _fox_prompt(proj3, ct, lw["fox_norm"], tq=tq)
    o_c = _sb_prompt(proj3, lw["sb_norm"], tq=tq)
    m = b * t
    x1 = _out_proj(x3.reshape(m, d), o_a.reshape(m, DN_W), o_b.reshape(m, FOX_W), o_c.reshape(m, SB_W),
                   lw["w_out"], layer, lw["norm_post_mix"], tm=512)
    y, st = _ffn_prompt(x1, t, lw["norm_pre_ffn"], lw["ffn_w_up"], lw["ffn_conv"], lw["ffn_w_down"], layer,
                        lw["norm_post_ffn"], tm=tm_ffn, tc=512)
    tps = t // tm_ffn
    ffn_buf_new = jnp.swapaxes(st[tps - 1::tps], 1, 2).reshape(b, FFN_CONV - 1, -1)
    _, _, logf, sk, sv = _new_rows(proj3, gates)
    fox_bufs = (_head_major_rows(proj3, BLK_FK, H_FOX, layer, depth, fox_bufs[0], tt=512),
                _head_major_rows(proj3, BLK_FV, H_FOX, layer, depth, fox_bufs[1], tt=512))
    return y.reshape(b, t, d), (s_new, dn_buf_new, logf, sk, sv, ffn_buf_new), fox_bufs


def _layer_sample(x3, lw, dn_s0, dn_buf, ffn_buf, pools, page_table, layer):
    b, t, d = x3.shape
    c = DN_CHUNK
    fox_k, fox_v, fox_lf, sb_k, sb_v = pools
    proj3, q, k, v, dn_buf_new, gates, cum, gcl = _mixers_common(x3, lw, dn_buf, tm=b * t, tt=t)
    padt = ((0, 0), (0, c - t), (0, 0))
    z = proj3[:, :, BLK_Z * LANE:BLK_Z * LANE + DN_W]
    gcl_pad = jnp.pad(gcl, padt, mode="edge")
    o_a, s_new = _gdn(jnp.pad(q, padt), jnp.pad(k, padt), jnp.pad(v, padt), jnp.pad(z, padt), 0,
                      jnp.pad(gates, padt), gcl_pad, _gclt(gcl_pad, c), dn_s0, lw["dn_norm"], nck=1)
    o_a = o_a[:, :t]
    cnt = jnp.swapaxes(cum[:, :, 2 * H_DN:2 * H_DN + H_FOX], 1, 2)
    cnt = jnp.pad(cnt, ((0, 0), (0, 8 - H_FOX), (0, PAGE - t)))
    o_b = _fox_dec(page_table, proj3, cnt, lw["fox_norm"], fox_k, fox_v, fox_lf, layer, g=PAGES_PER_STEP)
    o_c = _sb_dec(page_table, proj3, lw["sb_norm"], sb_k, sb_v, layer, g=PAGES_PER_STEP)
    m = b * t
    x1 = _out_proj(x3.reshape(m, d), o_a.reshape(m, DN_W), o_b.reshape(m, FOX_W), o_c.reshape(m, SB_W),
                   lw["w_out"], layer, lw["norm_post_mix"], tm=m)
    s0e = jnp.repeat(ffn_buf[:, 0, :], t, axis=0)
    s1e = jnp.repeat(ffn_buf[:, 1, :], t, axis=0)
    y, ug, uu = _ffn_dec(x1, t, lw["norm_pre_ffn"], lw["ffn_w_up"], lw["ffn_conv"], lw["ffn_w_down"], layer,
                         lw["norm_post_ffn"], s0e, s1e, tc=512)
    u = jnp.concatenate([ug, uu], axis=1).reshape(b, t, -1)
    ffn_buf_new = u[:, t - (FFN_CONV - 1):]
    fk, fv, logf, sk, sv = _new_rows(proj3, gates)
    return y.reshape(b, t, d), (s_new, dn_buf_new, fk, fv, logf, sk, sv, ffn_buf_new)


def kernel(x_prompt, x_sample, state_dn_S, state_dn_conv, cache_fox_k, cache_fox_v, cache_fox_logf, cache_sb_k, cache_sb_v, state_ffn_conv, page_table, norm_pre_mix, w_in, dn_conv, dn_A_log, dn_dt_bias, dn_norm, fox_b_f, fox_norm, sb_norm, w_out, norm_post_mix, norm_pre_ffn, ffn_w_up, ffn_conv, ffn_w_down, norm_post_ffn):
    depth = w_in.shape[0]
    yp, ys = x_prompt, x_sample
    prompt_new, sample_new = [], []
    fox_bufs = (None, None)
    lf_t = jnp.pad(jnp.swapaxes(cache_fox_logf, 2, 3), ((0, 0), (0, 0), (0, 8 - H_FOX), (0, 0)))
    pools = (jnp.swapaxes(cache_fox_k, 2, 3), jnp.swapaxes(cache_fox_v, 2, 3), lf_t, cache_sb_k, cache_sb_v)
    w_out_b, ffn_w_up_b, ffn_w_down_b = w_out.astype(BF16), ffn_w_up.astype(BF16), ffn_w_down.astype(BF16)
    for l in range(depth):
        lw = {
            "norm_pre_mix": norm_pre_mix[l][None, :],
            "w_in": _pack_w_in(w_in[l]),
            "dn_conv": dn_conv[l],
            "gate_bias": _lane_row([(H_DN, dn_dt_bias[l]), (2 * H_DN, fox_b_f[l])]),
            "alog_row": _lane_row([(H_DN, dn_A_log[l])]),
            "dn_norm": dn_norm[l][None, :],
            "fox_norm": fox_norm[l][None, :],
            "sb_norm": sb_norm[l][None, :],
            "w_out": w_out_b,
            "norm_post_mix": norm_post_mix[l][None, :],
            "norm_pre_ffn": norm_pre_ffn[l][None, :],
            "ffn_w_up": ffn_w_up_b,
            "ffn_conv": ffn_conv[l],
            "ffn_w_down": ffn_w_down_b,
            "norm_post_ffn": norm_post_ffn[l][None, :],
        }
        yp, st_p, fox_bufs = _layer_prompt(yp, lw, l, depth, fox_bufs)
        prompt_new.append(st_p)
        ys, st_s = _layer_sample(ys, lw, state_dn_S[l], state_dn_conv[l], state_ffn_conv[l], pools, page_table, l)
        sample_new.append(st_s)
    dn_s_p, dn_conv_p, logf_p, sk_p, sv_p, ffn_p = [jnp.stack(a) for a in zip(*prompt_new)]
    fk_p, fv_p = [jnp.swapaxes(x, 2, 3) for x in fox_bufs]
    s_out = [jnp.stack(a) for a in zip(*sample_new)]
    return (yp, ys, dn_s_p, dn_conv_p, fk_p, fv_p, logf_p, sk_p, sv_p, ffn_p, *s_out)
```
